```python
import math
import jax, jax.numpy as jnp
from jax import lax
import numpy as np

D_MODEL = 4096
BATCH = 4
SEQ = 2048
DEPTH = 1

MIX_WIDTH = 2 * D_MODEL
D_SSM = MIX_WIDTH // 2
SSM_HEADDIM = 64
SSM_HEADS = D_SSM // SSM_HEADDIM
SSM_GROUPS = 8
SSM_HEADS_PER_GROUP = SSM_HEADS // SSM_GROUPS
SSM_STATE = 128
SSM_CHUNK = 128
D_XBC = D_SSM + 2 * SSM_GROUPS * SSM_STATE
D_LRU = MIX_WIDTH - D_SSM
LRU_BLOCK = 256
LRU_HEADS = D_LRU // LRU_BLOCK
LRU_C = 8.0
CONV_WIDTH = 4
NORM_EPS = 1e-5
DEEPNORM_ALPHA = (2.0 * DEPTH) ** 0.25
DEEPNORM_BETA = (8.0 * DEPTH) ** -0.25
SPLIT_SIZES = (D_SSM, D_XBC, SSM_HEADS, D_LRU, D_LRU)
SPLIT_POINTS = tuple(int(v) for v in np.cumsum(SPLIT_SIZES)[:-1])
D_IN_PROJ = sum(SPLIT_SIZES)

kernel_name = "hymba_style_ssd_rglru_deepnorm"


def _causal_depthwise_conv(u, w, b):
    ch = u.shape[-1]
    y = lax.conv_general_dilated(
        u, w[:, None, :].astype(u.dtype), window_strides=(1,),
        padding=[(w.shape[0] - 1, 0)],
        dimension_numbers=("NWC", "WIO", "NWC"), feature_group_count=ch)
    return y + b.astype(u.dtype)


def _segsum(a):
    t = a.shape[-1]
    a_rep = jnp.broadcast_to(a[..., None], a.shape + (t,))
    strict = jnp.tril(jnp.ones((t, t), dtype=bool), -1)
    seg = jnp.cumsum(jnp.where(strict, a_rep, 0.0), axis=-2)
    lower = jnp.tril(jnp.ones((t, t), dtype=bool))
    return jnp.where(lower, seg, -jnp.inf)


def _ssd_chunked(xh, dt, a, bm, cm):
    bsz, s = xh.shape[0], xh.shape[1]
    nc = s // SSM_CHUNK
    g, r, p, n = SSM_GROUPS, SSM_HEADS_PER_GROUP, SSM_HEADDIM, SSM_STATE
    x = (xh * dt[..., None]).reshape(bsz, nc, SSM_CHUNK, g, r, p)
    a_dt = (dt * a).reshape(bsz, nc, SSM_CHUNK, g, r).transpose(0, 3, 4, 1, 2)
    bm = bm.reshape(bsz, nc, SSM_CHUNK, g, n)
    cm = cm.reshape(bsz, nc, SSM_CHUNK, g, n)
    a_cs = jnp.cumsum(a_dt, axis=-1)
    lmat = jnp.exp(_segsum(a_dt))
    cb = jnp.einsum("bclgn,bcsgn->bgcls", cm, bm)
    y_diag = jnp.einsum("bgcls,bgrcls,bcsgrp->bclgrp", cb, lmat, x)
    decay_states = jnp.exp(a_cs[..., -1:] - a_cs)
    states = jnp.einsum("bclgn,bgrcl,bclgrp->bcgrpn", bm, decay_states, x)
    chunk_tot = jnp.pad(a_cs[..., -1], [(0, 0), (0, 0), (0, 0), (1, 0)])
    decay_chunk = jnp.exp(_segsum(chunk_tot))
    states = jnp.concatenate([jnp.zeros_like(states[:, :1]), states], axis=1)
    prev_states = jnp.einsum("bgrzc,bcgrpn->bzgrpn", decay_chunk, states)[:, :-1]
    y_off = jnp.einsum("bclgn,bcgrpn,bgrcl->bclgrp", cm, prev_states, jnp.exp(a_cs))
    return (y_diag + y_off).reshape(bsz, s, g, r, p)


def _lin_combine(c1, c2):
    a1, b1 = c1
    a2, b2 = c2
    return (a1 * a2, a2 * b1 + b2)


def _hybrid_layer(x, w_in, ssd_conv_w, ssd_conv_b, ssd_dt_bias, ssd_a_log, ssd_d,
                  ssd_norm_w, lru_conv_w, lru_conv_b, lru_wa, lru_ba, lru_wx, lru_bx,
                  lru_lambda, w_out, ln_g, ln_b):
    dtype = x.dtype
    bsz, s, _ = x.shape
    f32 = jnp.float32
    proj = x @ w_in.astype(dtype)
    z, xbc, dt_raw, lx, lg = jnp.split(proj, SPLIT_POINTS, axis=-1)

    xbc = jax.nn.silu(_causal_depthwise_conv(xbc, ssd_conv_w, ssd_conv_b))
    xs, bm, cm = jnp.split(xbc, (D_SSM, D_SSM + SSM_GROUPS * SSM_STATE), axis=-1)
    g, r = SSM_GROUPS, SSM_HEADS_PER_GROUP
    xs_h = xs.astype(f32).reshape(bsz, s, g, r, SSM_HEADDIM)
    bm = bm.astype(f32).reshape(bsz, s, g, SSM_STATE)
    cm = cm.astype(f32).reshape(bsz, s, g, SSM_STATE)
    dt = jax.nn.softplus(dt_raw.astype(f32) + ssd_dt_bias.astype(f32)).reshape(bsz, s, g, r)
    a = -jnp.exp(ssd_a_log.astype(f32)).reshape(g, r)
    y = _ssd_chunked(xs_h, dt, a, bm, cm) + ssd_d.astype(f32).reshape(g, r)[..., None] * xs_h
    y = y.reshape(bsz, s, D_SSM) * jax.nn.silu(z.astype(f32))
    yg = y.reshape(bsz, s, SSM_GROUPS, D_SSM // SSM_GROUPS)
    yg = yg * lax.rsqrt(jnp.mean(yg * yg, axis=-1, keepdims=True) + NORM_EPS)
    ssd_out = (yg.reshape(bsz, s, D_SSM) * ssd_norm_w.astype(f32)).astype(dtype)

    u = _causal_depthwise_conv(lx, lru_conv_w, lru_conv_b)
    ub = u.reshape(bsz, s, LRU_HEADS, LRU_BLOCK)
    gate_r = jax.nn.sigmoid(jnp.einsum("bshi,hij->bshj", ub, lru_wa.astype(dtype)) + lru_ba.astype(dtype))
    gate_i = jax.nn.sigmoid(jnp.einsum("bshi,hij->bshj", ub, lru_wx.astype(dtype)) + lru_bx.astype(dtype))
    gate_r = gate_r.reshape(bsz, s, D_LRU).astype(f32)
    gate_i = gate_i.reshape(bsz, s, D_LRU).astype(f32)
    log_a = -LRU_C * jax.nn.softplus(-lru_lambda.astype(f32)) * gate_r
    a_t = jnp.exp(log_a)
    b_t = jnp.sqrt(-jnp.expm1(2.0 * log_a)) * (gate_i * u.astype(f32))
    _, h = lax.associative_scan(_lin_combine, (a_t, b_t), axis=1)
    lru_out = (h * jax.nn.silu(lg.astype(f32))).astype(dtype)

    mix = jnp.concatenate([ssd_out, lru_out], axis=-1)
    res = (DEEPNORM_ALPHA * x + mix @ w_out.astype(dtype)).astype(f32)
    mu = jnp.mean(res, axis=-1, keepdims=True)
    var = jnp.mean(jnp.square(res - mu), axis=-1, keepdims=True)
    out = (res - mu) * lax.rsqrt(var + NORM_EPS) * ln_g.astype(f32) + ln_b.astype(f32)
    return out.astype(dtype)


def setup_inputs(seed: int = 0) -> dict:
    key = jax.random.key(seed)
    ks = jax.random.split(key, 20)
    f32 = jnp.float32
    L = DEPTH
    x = jax.random.normal(ks[0], (BATCH, SEQ, D_MODEL), f32)
    w_in = jax.random.normal(ks[1], (L, D_MODEL, D_IN_PROJ), f32) * D_MODEL ** -0.5
    ssd_conv_w = jax.random.normal(ks[2], (L, CONV_WIDTH, D_XBC), f32) * CONV_WIDTH ** -0.5
    ssd_conv_b = jax.random.normal(ks[3], (L, D_XBC), f32) * 0.01
    dt0 = jnp.exp(jax.random.uniform(ks[4], (L, SSM_HEADS), f32)
                  * (math.log(0.1) - math.log(0.001)) + math.log(0.001))
    ssd_dt_bias = dt0 + jnp.log(-jnp.expm1(-dt0))
    ssd_a_log = jnp.log(jax.random.uniform(ks[5], (L, SSM_HEADS), f32, 1.0, 16.0))
    ssd_d = 1.0 + 0.01 * jax.random.normal(ks[6], (L, SSM_HEADS), f32)
    ssd_norm_w = 1.0 + 0.01 * jax.random.normal(ks[7], (L, D_SSM), f32)
    lru_conv_w = jax.random.normal(ks[8], (L, CONV_WIDTH, D_LRU), f32) * CONV_WIDTH ** -0.5
    lru_conv_b = jax.random.normal(ks[9], (L, D_LRU), f32) * 0.01
    lru_wa = jax.random.normal(ks[10], (L, LRU_HEADS, LRU_BLOCK, LRU_BLOCK), f32) * LRU_BLOCK ** -0.5
    lru_ba = jax.random.normal(ks[11], (L, LRU_HEADS, LRU_BLOCK), f32) * 0.01
    lru_wx = jax.random.normal(ks[12], (L, LRU_HEADS, LRU_BLOCK, LRU_BLOCK), f32) * LRU_BLOCK ** -0.5
    lru_bx = jax.random.normal(ks[13], (L, LRU_HEADS, LRU_BLOCK), f32) * 0.01
    a_c = jax.random.uniform(ks[14], (L, D_LRU), f32, 0.9, 0.999)
    a_base = a_c ** (1.0 / LRU_C)
    lru_lambda = jnp.log(a_base) - jnp.log1p(-a_base)
    w_out = jax.random.normal(ks[15], (L, MIX_WIDTH, D_MODEL), f32) * (MIX_WIDTH ** -0.5) * DEEPNORM_BETA
    ln_g = 1.0 + 0.01 * jax.random.normal(ks[16], (L, D_MODEL), f32)
    ln_b = 0.01 * jax.random.normal(ks[17], (L, D_MODEL), f32)
    return {"x": x, "w_in": w_in, "ssd_conv_w": ssd_conv_w, "ssd_conv_b": ssd_conv_b,
            "ssd_dt_bias": ssd_dt_bias, "ssd_a_log": ssd_a_log, "ssd_d": ssd_d,
            "ssd_norm_w": ssd_norm_w, "lru_conv_w": lru_conv_w, "lru_conv_b": lru_conv_b,
            "lru_wa": lru_wa, "lru_ba": lru_ba, "lru_wx": lru_wx, "lru_bx": lru_bx,
            "lru_lambda": lru_lambda, "w_out": w_out, "ln_g": ln_g, "ln_b": ln_b}


def reference(x, w_in, ssd_conv_w, ssd_conv_b, ssd_dt_bias, ssd_a_log, ssd_d, ssd_norm_w,
              lru_conv_w, lru_conv_b, lru_wa, lru_ba, lru_wx, lru_bx, lru_lambda,
              w_out, ln_g, ln_b):
    h = x
    for layer in range(DEPTH):
        h = _hybrid_layer(h, w_in[layer], ssd_conv_w[layer], ssd_conv_b[layer],
                          ssd_dt_bias[layer], ssd_a_log[layer], ssd_d[layer],
                          ssd_norm_w[layer], lru_conv_w[layer], lru_conv_b[layer],
                          lru_wa[layer], lru_ba[layer], lru_wx[layer], lru_bx[layer],
                          lru_lambda[layer], w_out[layer], ln_g[layer], ln_b[layer])
    return h
```

```python
import functools

import jax
import jax.numpy as jnp
from jax import lax
from jax.experimental import pallas as pl
from jax.experimental.pallas import tpu as pltpu

F32 = jnp.float32
BF16 = jnp.bfloat16

SSM_HEADDIM = 64
SSM_GROUPS = 8
SSM_STATE = 128
SSM_CHUNK = 128
LRU_BLOCK = 256
LRU_C = 8.0
CONV_WIDTH = 4
NORM_EPS = 1e-5
LOG2E = 1.4426950408889634

LANES = 128
SUBLANES = 8
MIB = 1024 * 1024
OUT_N_CHUNK = 512
LN_ROWS = 64
LRU_GROUP = SUBLANES * SUBLANES


def _sigmoid_tanh(v):
    return 0.5 * jnp.tanh(0.5 * v) + 0.5


def _softplus(v):
    return jnp.maximum(v, 0.0) + jnp.log1p(jnp.exp(-jnp.abs(v)))


def _mm_kernel(x_ref, w_ref, o_ref, wb_s, *, slab_out):
    @pl.when(pl.program_id(1) == 0)
    def _():
        wb_s[...] = w_ref[...].astype(BF16)

    res = lax.dot_general(x_ref[...], wb_s[...], (((1,), (1,)), ((), ())),
                          preferred_element_type=F32)
    if slab_out:
        for jj in range(o_ref.shape[0]):
            o_ref[jj] = res[:, jj * LANES:(jj + 1) * LANES]
    else:
        o_ref[...] = res


def _matmul(x, w_t, *, layer, row0, n_cols, tm, tn, slab_out=False):
    m, k = x.shape
    assert row0 % SUBLANES == 0 and n_cols % tn == 0 and m % tm == 0
    n_total = w_t.shape[1]
    w_spec = pl.BlockSpec((pl.Element(tn), pl.Element(k)),
                          lambda j, i: (pl.multiple_of(layer * n_total + row0 + j * tn,
                                                       SUBLANES), 0))
    if slab_out:
        out_spec = pl.BlockSpec((tn // LANES, tm, LANES), lambda j, i: (j, i, 0))
        out_shape = jax.ShapeDtypeStruct((n_cols // LANES, m, LANES), F32)
    else:
        out_spec = pl.BlockSpec((tm, tn), lambda j, i: (i, j))
        out_shape = jax.ShapeDtypeStruct((m, n_cols), F32)
    vmem = (2 * (tm * k * 2 + tn * k * 4 + tm * tn * 4) + tn * k * 2 + tm * tn * 4
            + 6 * MIB)
    return pl.pallas_call(
        functools.partial(_mm_kernel, slab_out=slab_out),
        grid=(n_cols // tn, m // tm),
        in_specs=[pl.BlockSpec((tm, k), lambda j, i: (i, 0)), w_spec],
        out_specs=out_spec,
        out_shape=out_shape,
        scratch_shapes=[pltpu.VMEM((tn, k), BF16)],
        compiler_params=pltpu.CompilerParams(
            dimension_semantics=("arbitrary", "arbitrary"),
            vmem_limit_bytes=vmem),
        name="in_proj",
    )(x, w_t.reshape(-1, k))


def _ssd_kernel(z_ref, xs_ref, b_ref, c_ref, dt_ref,
                cwx_ref, cwb_ref, cwc_ref, cbx_ref, cbb_ref, cbc_ref,
                dtb_ref, alog_ref, d_ref, nw_ref,
                o_ref,
                dt_s, acs_s, acst_s, state_s, tail_s, ext_s):
    c = pl.program_id(1)
    g = pl.program_id(2)
    L = SSM_CHUNK
    gw = xs_ref.shape[1]
    heads_per_group = gw // SSM_HEADDIM
    pairs = gw // LANES

    row_id = lax.broadcasted_iota(jnp.int32, (L, LANES), 0)
    lane_id = lax.broadcasted_iota(jnp.int32, (L, LANES), 1)

    @pl.when(g == 0)
    def _():
        dt = _softplus(dt_ref[...] + dtb_ref[...])
        a_dt = dt * (-jnp.exp(alog_ref[...]))
        acs = a_dt
        k = 1
        while k < L:
            acs = acs + jnp.where(row_id >= k, pltpu.roll(acs, k, axis=0), 0.0)
            k *= 2
        acs2 = acs * LOG2E
        dt_s[...] = dt
        acs_s[...] = acs2
        acst_s[...] = acs2.T

    @pl.when(c == 0)
    def _():
        state_s[g] = jnp.zeros(state_s.shape[1:], F32)
        tail_s[g] = jnp.zeros(tail_s.shape[1:], F32)

    n_xs = gw // LANES
    acts = []
    for s in range(n_xs + 2):
        if s < n_xs:
            lanes = slice(s * LANES, (s + 1) * LANES)
            raw, w_ref, bias_ref = xs_ref[:, lanes], cwx_ref, cbx_ref
        else:
            lanes = slice(0, LANES)
            raw, w_ref, bias_ref = ((b_ref[...], cwb_ref, cbb_ref) if s == n_xs
                                    else (c_ref[...], cwc_ref, cbc_ref))
        ext_s[s, 0:SUBLANES, :] = tail_s[g, :, s * LANES:(s + 1) * LANES]
        ext_s[s, SUBLANES:SUBLANES + L, :] = raw
        tail_s[g, :, s * LANES:(s + 1) * LANES] = raw[L - SUBLANES:, :]
        acc = raw * w_ref[CONV_WIDTH - 1:CONV_WIDTH, lanes] + bias_ref[:, lanes]
        for k in range(1, CONV_WIDTH):
            acc = acc + (ext_s[s, pl.ds(SUBLANES - k, L), :]
                         * w_ref[CONV_WIDTH - 1 - k:CONV_WIDTH - k, lanes])
        half = 0.5 * acc
        acts.append(half + half * jnp.tanh(half))
    xs = jnp.concatenate(acts[:n_xs], axis=1)
    bm = acts[n_xs].astype(BF16)
    cm = acts[n_xs + 1].astype(BF16)

    cb_mat = lax.dot_general(cm, bm, (((1,), (1,)), ((), ())),
                             preferred_element_type=F32)
    s_prev = state_s[g]
    y_off = jnp.dot(cm, s_prev.astype(BF16), preferred_element_type=F32)

    dt_all = dt_s[...]
    acs_all = acs_s[...]
    causal = row_id >= lane_id
    low = lane_id < SSM_HEADDIM

    def head_col(arr, idx):
        return jnp.sum(jnp.where(lane_id == idx, arr, 0.0), axis=1, keepdims=True)

    def head_pair(arr, idx0):
        return jnp.where(low, head_col(arr, idx0), head_col(arr, idx0 + 1))

    y_parts, xdec_parts, cdec_parts = [], [], []
    for j in range(pairs):
        i0 = g * heads_per_group + 2 * j
        i1 = i0 + 1
        col0 = head_col(acs_all, i0)
        col1 = head_col(acs_all, i1)
        row0 = acst_s[pl.ds(i0, 1), :]
        row1 = acst_s[pl.ds(i1, 1), :]
        l0 = jnp.where(causal, jnp.exp2(col0 - row0), 0.0)
        l1 = jnp.where(causal, jnp.exp2(col1 - row1), 0.0)
        m_pair = jnp.concatenate([cb_mat * l0, cb_mat * l1], axis=1).astype(BF16)

        dt_pair = head_pair(dt_all, i0)
        xs_pair = xs[:, j * LANES:(j + 1) * LANES]
        xt = xs_pair * dt_pair
        xt_b = xt.astype(BF16)
        zero = jnp.zeros_like(xt_b)
        rhs = jnp.concatenate([jnp.where(low, xt_b, zero),
                               jnp.where(low, zero, xt_b)], axis=0)
        y_diag = jnp.dot(m_pair, rhs, preferred_element_type=F32)

        scale = jnp.where(low, jnp.exp2(col0), jnp.exp2(col1))
        last0 = col0[L - 1:L, :]
        last1 = col1[L - 1:L, :]
        dec = jnp.where(low, jnp.exp2(last0 - col0), jnp.exp2(last1 - col1))
        y_parts.append(y_diag + y_off[:, j * LANES:(j + 1) * LANES] * scale
                       + d_ref[:, j * LANES:(j + 1) * LANES] * xs_pair)
        xdec_parts.append((xt * dec).astype(BF16))
        cdec_parts.append(scale[L - 1:L, :])

    xdec = jnp.concatenate(xdec_parts, axis=1)
    cdec = jnp.concatenate(cdec_parts, axis=1)
    new_states = lax.dot_general(bm, xdec, (((0,), (0,)), ((), ())),
                                 preferred_element_type=F32)
    state_s[g] = s_prev * cdec + new_states

    y = jnp.concatenate(y_parts, axis=1)
    zh = 0.5 * z_ref[...]
    y = y * (zh + zh * jnp.tanh(zh))
    ms = jnp.mean(y * y, axis=1, keepdims=True)
    o_ref[...] = (y * lax.rsqrt(ms + NORM_EPS) * nw_ref[...]).astype(o_ref.dtype)


def _ssd_mixer(proj_a, dt_raw, conv_w, conv_b, dt_bias, a_log, d_exp, norm_w,
               *, batch, seq, d_ssm):
    L = SSM_CHUNK
    gw = d_ssm // SSM_GROUPS
    nchunk = seq // L
    t = batch * seq
    xs0 = d_ssm // gw
    b0 = 2 * d_ssm // SSM_STATE
    c0 = b0 + SSM_GROUPS
    cwb0 = d_ssm // SSM_STATE
    cwc0 = cwb0 + SSM_GROUPS

    def rows(b, c, g):
        return b * nchunk + c

    in_specs = [
        pl.BlockSpec((L, gw), lambda b, c, g: (rows(b, c, g), g)),
        pl.BlockSpec((L, gw), lambda b, c, g: (rows(b, c, g), xs0 + g)),
        pl.BlockSpec((L, SSM_STATE), lambda b, c, g: (rows(b, c, g), b0 + g)),
        pl.BlockSpec((L, SSM_STATE), lambda b, c, g: (rows(b, c, g), c0 + g)),
        pl.BlockSpec((L, LANES), lambda b, c, g: (rows(b, c, g), 0)),
        pl.BlockSpec((CONV_WIDTH, gw), lambda b, c, g: (0, g)),
        pl.BlockSpec((CONV_WIDTH, SSM_STATE), lambda b, c, g: (0, cwb0 + g)),
        pl.BlockSpec((CONV_WIDTH, SSM_STATE), lambda b, c, g: (0, cwc0 + g)),
        pl.BlockSpec((1, gw), lambda b, c, g: (0, g)),
        pl.BlockSpec((1, SSM_STATE), lambda b, c, g: (0, cwb0 + g)),
        pl.BlockSpec((1, SSM_STATE), lambda b, c, g: (0, cwc0 + g)),
        pl.BlockSpec((1, LANES), lambda b, c, g: (0, 0)),
        pl.BlockSpec((1, LANES), lambda b, c, g: (0, 0)),
        pl.BlockSpec((1, gw), lambda b, c, g: (0, g)),
        pl.BlockSpec((1, gw), lambda b, c, g: (0, g)),
    ]
    return pl.pallas_call(
        _ssd_kernel,
        grid=(batch, nchunk, SSM_GROUPS),
        in_specs=in_specs,
        out_specs=pl.BlockSpec((L, gw), lambda b, c, g: (rows(b, c, g), g)),
        out_shape=jax.ShapeDtypeStruct((t, d_ssm), BF16),
        scratch_shapes=[
            pltpu.VMEM((L, LANES), F32),
            pltpu.VMEM((L, LANES), F32),
            pltpu.VMEM((LANES, L), F32),
            pltpu.VMEM((SSM_GROUPS, SSM_STATE, gw), F32),
            pltpu.VMEM((SSM_GROUPS, SUBLANES, gw + 2 * SSM_STATE), F32),
            pltpu.VMEM((gw // LANES + 2, SUBLANES + L, LANES), F32),
        ],
        compiler_params=pltpu.CompilerParams(
            dimension_semantics=("arbitrary", "arbitrary", "arbitrary"),
            vmem_limit_bytes=32 * MIB),
        name="ssd_mixer",
    )(proj_a, proj_a, proj_a, proj_a, dt_raw,
      conv_w, conv_w, conv_w, conv_b, conv_b, conv_b,
      dt_bias, a_log, d_exp, norm_w)


def _lru_kernel(lx_ref, lg_ref, cw_ref, cb_ref, wa_ref, ba_ref, wx_ref, bx_ref,
                lam_ref, o_ref, tail_s, h_s):
    tb = pl.program_id(2)
    n_slab, ts, _ = lx_ref.shape
    n_group = ts // LRU_GROUP
    slabs_per_head = LRU_BLOCK // LANES
    S = SUBLANES
    vshape = (S, LANES)

    @pl.when(tb == 0)
    def _():
        tail_s[...] = jnp.zeros(tail_s.shape, F32)
        h_s[...] = jnp.zeros(h_s.shape, F32)

    sub = lax.broadcasted_iota(jnp.int32, vshape, 0)
    seg0 = sub == 0

    def seg_rows(ref, jj, row0):
        return ref[jj, pl.ds(row0, S, stride=S), :]

    u = []
    for jj in range(n_slab):
        lanes = slice(jj * LANES, (jj + 1) * LANES)
        taps = [jnp.broadcast_to(cw_ref[k:k + 1, lanes], vshape) for k in range(CONV_WIDTH)]
        bias = jnp.broadcast_to(cb_ref[:, lanes], vshape)
        u_slab = []
        for g in range(n_group):
            g0 = g * LRU_GROUP
            x_t = [seg_rows(lx_ref, jj, g0 + t) for t in range(S)]
            before = []
            for d in range(CONV_WIDTH - 1, 0, -1):
                if g == 0:
                    prev_row = jnp.broadcast_to(tail_s[jj, S - d:S - d + 1, :], vshape)
                    before.append(jnp.where(seg0, prev_row,
                                            pltpu.roll(x_t[S - d], 1, axis=0)))
                else:
                    before.append(seg_rows(lx_ref, jj, g0 - d))
            ext = before + x_t
            u_g = []
            for t in range(S):
                acc = ext[t + 3] * taps[3] + bias
                for k in range(CONV_WIDTH - 1):
                    acc = acc + ext[t + k] * taps[k]
                u_g.append(acc)
            u_slab.append(u_g)
        tail_s[jj] = lx_ref[jj, ts - S:ts, :]
        u.append(u_slab)

    pre_r, pre_i = [], []
    for h in range(n_slab // slabs_per_head):
        lhs = jnp.concatenate(
            [jnp.concatenate([u[h * slabs_per_head + q][g][t] for q in range(slabs_per_head)],
                             axis=1)
             for g in range(n_group) for t in range(S)], axis=0).astype(BF16)
        pre_r.append(jnp.dot(lhs, wa_ref[h], preferred_element_type=F32))
        pre_i.append(jnp.dot(lhs, wx_ref[h], preferred_element_type=F32))

    for jj in range(n_slab):
        lanes = slice(jj * LANES, (jj + 1) * LANES)
        h_idx, q = divmod(jj, slabs_per_head)
        qlanes = slice(q * LANES, (q + 1) * LANES)
        coef = jnp.broadcast_to(-LRU_C * _softplus(-lam_ref[:, lanes]), vshape)
        b_r = jnp.broadcast_to(ba_ref[:, lanes], vshape)
        b_i = jnp.broadcast_to(bx_ref[:, lanes], vshape)
        carry = h_s[jj]
        for g in range(n_group):
            g0 = g * LRU_GROUP
            a_t, h_t = [], []
            for t in range(S):
                r0 = g0 + t * S
                gate_r = _sigmoid_tanh(pre_r[h_idx][r0:r0 + S, qlanes] + b_r)
                gate_i = _sigmoid_tanh(pre_i[h_idx][r0:r0 + S, qlanes] + b_i)
                log_a = coef * gate_r
                a = jnp.exp(log_a)
                th = jnp.tanh(log_a)
                bv = jnp.sqrt((-2.0 * th) / (1.0 - th)) * (gate_i * u[jj][g][t])
                if t == 0:
                    a_t.append(a)
                    h_t.append(bv)
                else:
                    a_t.append(a * a_t[-1])
                    h_t.append(a * h_t[-1] + bv)
            a_e, h_e = a_t[-1], h_t[-1]
            k = 1
            while k < S:
                keep = sub >= k
                a_sh = jnp.where(keep, pltpu.roll(a_e, k, axis=0), 1.0)
                h_sh = jnp.where(keep, pltpu.roll(h_e, k, axis=0), 0.0)
                h_e = a_e * h_sh + h_e
                a_e = a_e * a_sh
                k *= 2
            ends = h_e + a_e * carry
            seg_in = jnp.where(seg0, carry, pltpu.roll(ends, 1, axis=0))
            carry = jnp.broadcast_to(ends[S - 1:S, :], vshape)
            for t in range(S):
                hv = h_t[t] + a_t[t] * seg_in
                lg = seg_rows(lg_ref, jj, g0 + t)
                o_ref[jj, pl.ds(g0 + t, S, stride=S), :] = hv * (lg * _sigmoid_tanh(lg))
        h_s[jj] = carry


def _lru_mixer(proj_l, conv_w, conv_b, wa, ba, wx, bx, lam, *, batch, seq, d_lru, ts, cbw):
    t = batch * seq
    nt = seq // ts
    ncb = d_lru // cbw
    n_slab = cbw // LANES
    hb = cbw // LRU_BLOCK
    assert ts % LRU_GROUP == 0 and seq % ts == 0
    vec = pl.BlockSpec((1, cbw), lambda b, j, s: (0, j))
    return pl.pallas_call(
        _lru_kernel,
        grid=(batch, ncb, nt),
        in_specs=[
            pl.BlockSpec((n_slab, ts, LANES), lambda b, j, s: (j, b * nt + s, 0)),
            pl.BlockSpec((n_slab, ts, LANES), lambda b, j, s: (ncb + j, b * nt + s, 0)),
            pl.BlockSpec((CONV_WIDTH, cbw), lambda b, j, s: (0, j)),
            vec,
            pl.BlockSpec((hb, LRU_BLOCK, LRU_BLOCK), lambda b, j, s: (j, 0, 0)),
            vec,
            pl.BlockSpec((hb, LRU_BLOCK, LRU_BLOCK), lambda b, j, s: (j, 0, 0)),
            vec,
            vec,
        ],
        out_specs=pl.BlockSpec((n_slab, ts, LANES), lambda b, j, s: (j, b * nt + s, 0)),
        out_shape=jax.ShapeDtypeStruct((d_lru // LANES, t, LANES), F32),
        scratch_shapes=[pltpu.VMEM((n_slab, SUBLANES, LANES), F32),
                        pltpu.VMEM((n_slab, SUBLANES, LANES), F32)],
        compiler_params=pltpu.CompilerParams(
            dimension_semantics=("arbitrary", "arbitrary", "arbitrary"),
            vmem_limit_bytes=32 * MIB),
        name="lru_mixer",
    )(proj_l, proj_l, conv_w, conv_b, wa, ba, wx, bx, lam)


def _out_kernel(ssd_ref, lru_ref, w_ref, x_hbm, g_ref, b_ref, o_ref, xbuf, sem,
                *, nk_half, alpha):
    i = pl.program_id(0)
    k = pl.program_id(1)
    tm, d_model = o_ref.shape

    def x_copy():
        return pltpu.make_async_copy(x_hbm.at[pl.ds(i * tm, tm), :], xbuf, sem)

    def accumulate(lhs, first):
        for n in range(0, d_model, OUT_N_CHUNK):
            sl = slice(n, n + OUT_N_CHUNK)
            part = jnp.dot(lhs, w_ref[:, sl], preferred_element_type=F32)
            if first:
                o_ref[:, sl] = part
            else:
                o_ref[:, sl] += part

    @pl.when(k == 0)
    def _():
        x_copy().start()
        accumulate(ssd_ref[...], True)

    @pl.when(jnp.logical_and(k > 0, k < nk_half))
    def _():
        accumulate(ssd_ref[...], False)

    @pl.when(k >= nk_half)
    def _():
        lhs = jnp.concatenate([lru_ref[jj] for jj in range(lru_ref.shape[0])], axis=1)
        accumulate(lhs.astype(BF16), False)

    @pl.when(k == 2 * nk_half - 1)
    def _():
        x_copy().wait()

        def ln_rows(r, carry):
            rows = pl.ds(pl.multiple_of(r * LN_ROWS, LN_ROWS), LN_ROWS)
            res = o_ref[rows, :] + alpha * xbuf[rows, :]
            mu = jnp.mean(res, axis=1, keepdims=True)
            cen = res - mu
            var = jnp.mean(cen * cen, axis=1, keepdims=True)
            o_ref[rows, :] = cen * lax.rsqrt(var + NORM_EPS) * g_ref[...] + b_ref[...]
            return carry

        lax.fori_loop(0, tm // LN_ROWS, ln_rows, 0)


def _out_proj(ssd_out, lru_out, w_out, x2d, ln_g, ln_b, *, alpha, tm, tk):
    t, d_half = ssd_out.shape
    d_model = x2d.shape[1]
    nk_half = d_half // tk
    ks = tk // LANES
    vmem = (2 * (tm * tk * 2 + tm * tk * 4 + tk * d_model * 2 + tm * d_model * 4)
            + tm * d_model * 4 + 8 * MIB)
    return pl.pallas_call(
        functools.partial(_out_kernel, nk_half=nk_half, alpha=alpha),
        grid=(t // tm, 2 * nk_half),
        in_specs=[
            pl.BlockSpec((tm, tk), lambda i, k: (i, jnp.minimum(k, nk_half - 1))),
            pl.BlockSpec((ks, tm, LANES), lambda i, k: (jnp.maximum(k - nk_half, 0), i, 0)),
            pl.BlockSpec((tk, d_model), lambda i, k: (k, 0)),
            pl.BlockSpec(memory_space=pl.ANY),
            pl.BlockSpec((1, d_model), lambda i, k: (0, 0)),
            pl.BlockSpec((1, d_model), lambda i, k: (0, 0)),
        ],
        out_specs=pl.BlockSpec((tm, d_model), lambda i, k: (i, 0)),
        out_shape=jax.ShapeDtypeStruct((t, d_model), F32),
        scratch_shapes=[pltpu.VMEM((tm, d_model), F32), pltpu.SemaphoreType.DMA(())],
        compiler_params=pltpu.CompilerParams(
            dimension_semantics=("arbitrary", "arbitrary"),
            vmem_limit_bytes=vmem),
        name="out_proj_ln",
    )(ssd_out, lru_out, w_out, x2d, ln_g, ln_b)


def _layer(x2d, layer, w_in, ssd_conv_w, ssd_conv_b, ssd_dt_bias, ssd_a_log, ssd_d,
           ssd_norm_w, lru_conv_w, lru_conv_b, lru_wa, lru_ba, lru_wx, lru_bx, lru_lambda,
           w_out, ln_g, ln_b, *, batch, seq, alpha):
    t, d_model = x2d.shape
    d_ssm = d_model
    d_lru = d_model
    heads = d_ssm // SSM_HEADDIM
    d_xbc = d_ssm + 2 * SSM_GROUPS * SSM_STATE
    n_a = d_ssm + d_xbc
    assert heads <= LANES and seq % SSM_CHUNK == 0

    xb = x2d.astype(BF16)
    w_t = jnp.swapaxes(w_in, 1, 2)

    tm = min(1024, t)
    proj_a = _matmul(xb, w_t, layer=layer, row0=0, n_cols=n_a, tm=tm, tn=512)
    dt_raw = _matmul(xb, w_t, layer=layer, row0=n_a, n_cols=LANES, tm=tm, tn=LANES)
    proj_l = _matmul(xb, w_t, layer=layer, row0=n_a + heads, n_cols=2 * d_lru, tm=tm,
                     tn=512, slab_out=True)

    pad_h = (0, LANES - heads)
    ssd_out = _ssd_mixer(
        proj_a, dt_raw, ssd_conv_w, ssd_conv_b.reshape(1, d_xbc),
        jnp.pad(ssd_dt_bias, pad_h).reshape(1, LANES),
        jnp.pad(ssd_a_log, pad_h).reshape(1, LANES),
        jnp.repeat(ssd_d, SSM_HEADDIM).reshape(1, d_ssm),
        ssd_norm_w.reshape(1, d_ssm),
        batch=batch, seq=seq, d_ssm=d_ssm)

    lru_out = _lru_mixer(
        proj_l, lru_conv_w, lru_conv_b.reshape(1, d_lru),
        lru_wa.astype(BF16), lru_ba.reshape(1, d_lru),
        lru_wx.astype(BF16), lru_bx.reshape(1, d_lru),
        lru_lambda.reshape(1, d_lru),
        batch=batch, seq=seq, d_lru=d_lru, ts=min(512, seq), cbw=512)

    return _out_proj(ssd_out, lru_out, w_out.astype(BF16), x2d,
                     ln_g.reshape(1, d_model), ln_b.reshape(1, d_model),
                     alpha=alpha, tm=min(512, t), tk=1024)


def kernel(x, w_in, ssd_conv_w, ssd_conv_b, ssd_dt_bias, ssd_a_log, ssd_d, ssd_norm_w,
           lru_conv_w, lru_conv_b, lru_wa, lru_ba, lru_wx, lru_bx, lru_lambda,
           w_out, ln_g, ln_b):
    batch, seq, d_model = x.shape
    depth = w_in.shape[0]
    alpha = (2.0 * depth) ** 0.25
    h = x.reshape(batch * seq, d_model)
    for layer in range(depth):
        h = _layer(h, layer, w_in, ssd_conv_w[layer], ssd_conv_b[layer],
                   ssd_dt_bias[layer], ssd_a_log[layer], ssd_d[layer], ssd_norm_w[layer],
                   lru_conv_w[layer], lru_conv_b[layer], lru_wa[layer], lru_ba[layer],
                   lru_wx[layer], lru_bx[layer], lru_lambda[layer],
                   w_out[layer], ln_g[layer], ln_b[layer],
                   batch=batch, seq=seq, alpha=alpha)
    return h.reshape(batch, seq, d_model)
```

```python
import functools

import jax
import jax.numpy as jnp
from jax import lax
from jax.experimental import pallas as pl
from jax.experimental.pallas import tpu as pltpu

F32 = jnp.float32
BF16 = jnp.bfloat16

SSM_HEADDIM = 64
SSM_GROUPS = 8
SSM_STATE = 128
SSM_CHUNK = 128
LRU_BLOCK = 256
LRU_C = 8.0
CONV_WIDTH = 4
NORM_EPS = 1e-5
LOG2E = 1.4426950408889634

LANES = 128
SUBLANES = 8
MIB = 1024 * 1024
OUT_N_CHUNK = 512
LN_ROWS = 64
LRU_GROUP = SUBLANES * SUBLANES
SSD_GROUPS_PER_STEP = 8


def _softplus(v):
    return jnp.maximum(v, 0.0) + jnp.log1p(jnp.exp(-jnp.abs(v)))


def _mm_kernel(x_ref, w_hbm, o_ref, wb_s, stage_s, sem, *, slab_out, row0, n_j, n_i):
    j = pl.program_id(0)
    i = pl.program_id(1)
    tn = wb_s.shape[1]
    ch = stage_s.shape[0]

    def chunk_copy(block, c):
        start = pl.multiple_of(row0 + block * tn + c * ch, SUBLANES)
        return pltpu.make_async_copy(w_hbm.at[pl.ds(start, ch), :], stage_s, sem)

    def land(slot, c):
        rows = pl.ds(pl.multiple_of(c * ch, ch), ch)
        wb_s[slot, rows, :] = stage_s[...].astype(BF16)

    @pl.when(jnp.logical_and(j == 0, i == 0))
    def _():
        def first_block(c, carry):
            copy = chunk_copy(0, c)
            copy.start()
            copy.wait()
            land(0, c)
            return carry

        lax.fori_loop(0, n_i, first_block, 0)

    cur = j % 2
    has_next = j + 1 < n_j

    @pl.when(has_next)
    def _():
        chunk_copy(j + 1, i).start()

    res = lax.dot_general(x_ref[...], wb_s[cur], (((1,), (1,)), ((), ())),
                          preferred_element_type=F32)

    @pl.when(has_next)
    def _():
        chunk_copy(j + 1, i).wait()
        land(1 - cur, i)

    if slab_out:
        for jj in range(o_ref.shape[0]):
            o_ref[jj] = res[:, jj * LANES:(jj + 1) * LANES]
    else:
        o_ref[...] = res


def _matmul(x, w_t, *, layer, row0, n_cols, tm, tn, slab_out=False):
    m, k = x.shape
    n_j, n_i = n_cols // tn, m // tm
    ch = tn // n_i
    bf16_rows = 2 * SUBLANES
    assert n_cols % tn == 0 and m % tm == 0 and tn % n_i == 0 and ch % bf16_rows == 0
    assert row0 % SUBLANES == 0
    row0 = layer * w_t.shape[1] + row0
    if slab_out:
        out_spec = pl.BlockSpec((tn // LANES, tm, LANES), lambda j, i: (j, i, 0))
        out_shape = jax.ShapeDtypeStruct((n_cols // LANES, m, LANES), F32)
    else:
        out_spec = pl.BlockSpec((tm, tn), lambda j, i: (i, j))
        out_shape = jax.ShapeDtypeStruct((m, n_cols), F32)
    vmem = (2 * (tm * k * 2 + tn * k * 2 + tm * tn * 4) + ch * k * 4 + tm * tn * 4
            + 6 * MIB)
    return pl.pallas_call(
        functools.partial(_mm_kernel, slab_out=slab_out, row0=row0, n_j=n_j, n_i=n_i),
        grid=(n_j, n_i),
        in_specs=[pl.BlockSpec((tm, k), lambda j, i: (i, 0)),
                  pl.BlockSpec(memory_space=pl.ANY)],
        out_specs=out_spec,
        out_shape=out_shape,
        scratch_shapes=[pltpu.VMEM((2, tn, k), BF16),
                        pltpu.VMEM((ch, k), F32),
                        pltpu.SemaphoreType.DMA(())],
        compiler_params=pltpu.CompilerParams(
            dimension_semantics=("arbitrary", "arbitrary"),
            vmem_limit_bytes=vmem),
        name="in_proj",
    )(x, w_t.reshape(-1, k))


def _ssd_kernel(z_ref, xs_ref, b_ref, c_ref, dt_ref,
                cwx_ref, cwb_ref, cwc_ref, cbx_ref, cbb_ref, cbc_ref,
                dtb_ref, alog_ref, d_ref, nw_ref,
                o_ref,
                dt_s, acs_s, acst_s, state_s, tail_s, ext_s):
    c = pl.program_id(1)
    gstep = pl.program_id(2)
    L = SSM_CHUNK
    gs = SSD_GROUPS_PER_STEP
    gw = xs_ref.shape[1] // gs

    row_id = lax.broadcasted_iota(jnp.int32, (L, LANES), 0)

    @pl.when(gstep == 0)
    def _():
        dt = _softplus(dt_ref[...] + dtb_ref[...])
        a_dt = dt * (-jnp.exp(alog_ref[...]))
        acs = a_dt
        k = 1
        while k < L:
            acs = acs + jnp.where(row_id >= k, pltpu.roll(acs, k, axis=0), 0.0)
            k *= 2
        acs2 = acs * LOG2E
        dt_s[...] = dt
        acs_s[...] = acs2
        acst_s[...] = acs2.T

    for gg in range(gs):
        _ssd_group(gg, gstep * gs + gg, c, gw,
                   z_ref, xs_ref, b_ref, c_ref, cwx_ref, cwb_ref, cwc_ref,
                   cbx_ref, cbb_ref, cbc_ref, d_ref, nw_ref, o_ref,
                   dt_s, acs_s, acst_s, state_s, tail_s, ext_s)


def _ssd_group(gg, g, c, gw, z_ref, xs_ref, b_ref, c_ref, cwx_ref, cwb_ref, cwc_ref,
               cbx_ref, cbb_ref, cbc_ref, d_ref, nw_ref, o_ref,
               dt_s, acs_s, acst_s, state_s, tail_s, ext_s):
    L = SSM_CHUNK
    heads_per_group = gw // SSM_HEADDIM
    pairs = gw // LANES
    n_xs = gw // LANES
    x0 = gg * gw
    bc0 = gg * SSM_STATE
    row_id = lax.broadcasted_iota(jnp.int32, (L, LANES), 0)
    lane_id = lax.broadcasted_iota(jnp.int32, (L, LANES), 1)

    @pl.when(c == 0)
    def _():
        state_s[g] = jnp.zeros(state_s.shape[1:], F32)
        tail_s[g] = jnp.zeros(tail_s.shape[1:], F32)

    acts = []
    for s in range(n_xs + 2):
        if s < n_xs:
            lanes = slice(x0 + s * LANES, x0 + (s + 1) * LANES)
            in_ref, w_ref, bias_ref = xs_ref, cwx_ref, cbx_ref
        else:
            lanes = slice(bc0, bc0 + LANES)
            in_ref, w_ref, bias_ref = ((b_ref, cwb_ref, cbb_ref) if s == n_xs
                                       else (c_ref, cwc_ref, cbc_ref))
        raw = in_ref[:, lanes]
        slab = gg * (n_xs + 2) + s
        ext_s[slab, 0:SUBLANES, :] = tail_s[g, :, s * LANES:(s + 1) * LANES]
        ext_s[slab, SUBLANES:SUBLANES + L, :] = raw
        tail_s[g, :, s * LANES:(s + 1) * LANES] = raw[L - SUBLANES:, :]
        acc = raw * w_ref[CONV_WIDTH - 1:CONV_WIDTH, lanes] + bias_ref[:, lanes]
        for k in range(1, CONV_WIDTH):
            acc = acc + (ext_s[slab, pl.ds(SUBLANES - k, L), :]
                         * w_ref[CONV_WIDTH - 1 - k:CONV_WIDTH - k, lanes])
        half = 0.5 * acc
        acts.append(half + half * jnp.tanh(half))
    xs = jnp.concatenate(acts[:n_xs], axis=1)
    bm = acts[n_xs].astype(BF16)
    cm = acts[n_xs + 1].astype(BF16)

    cb_mat = lax.dot_general(cm, bm, (((1,), (1,)), ((), ())),
                             preferred_element_type=F32)
    s_prev = state_s[g]
    y_off = jnp.dot(cm, s_prev.astype(BF16), preferred_element_type=F32)

    dt_all = dt_s[...]
    acs_all = acs_s[...]
    causal = row_id >= lane_id
    low = lane_id < SSM_HEADDIM

    def head_col(arr, idx):
        return jnp.sum(jnp.where(lane_id == idx, arr, 0.0), axis=1, keepdims=True)

    def head_pair(arr, idx0):
        return jnp.where(low, head_col(arr, idx0), head_col(arr, idx0 + 1))

    y_parts, xdec_parts, cdec_parts = [], [], []
    for j in range(pairs):
        i0 = g * heads_per_group + 2 * j
        i1 = i0 + 1
        col0 = head_col(acs_all, i0)
        col1 = head_col(acs_all, i1)
        row0 = acst_s[pl.ds(i0, 1), :]
        row1 = acst_s[pl.ds(i1, 1), :]
        l0 = jnp.where(causal, jnp.exp2(col0 - row0), 0.0)
        l1 = jnp.where(causal, jnp.exp2(col1 - row1), 0.0)
        m_pair = jnp.concatenate([cb_mat * l0, cb_mat * l1], axis=1).astype(BF16)

        dt_pair = head_pair(dt_all, i0)
        xs_pair = xs[:, j * LANES:(j + 1) * LANES]
        xt = xs_pair * dt_pair
        xt_b = xt.astype(BF16)
        zero = jnp.zeros_like(xt_b)
        rhs = jnp.concatenate([jnp.where(low, xt_b, zero),
                               jnp.where(low, zero, xt_b)], axis=0)
        y_diag = jnp.dot(m_pair, rhs, preferred_element_type=F32)

        scale = jnp.where(low, jnp.exp2(col0), jnp.exp2(col1))
        last0 = col0[L - 1:L, :]
        last1 = col1[L - 1:L, :]
        dec = jnp.where(low, jnp.exp2(last0 - col0), jnp.exp2(last1 - col1))
        y_parts.append(y_diag + y_off[:, j * LANES:(j + 1) * LANES] * scale
                       + d_ref[:, x0 + j * LANES:x0 + (j + 1) * LANES] * xs_pair)
        xdec_parts.append((xt * dec).astype(BF16))
        cdec_parts.append(scale[L - 1:L, :])

    xdec = jnp.concatenate(xdec_parts, axis=1)
    cdec = jnp.concatenate(cdec_parts, axis=1)
    new_states = lax.dot_general(bm, xdec, (((0,), (0,)), ((), ())),
                                 preferred_element_type=F32)
    state_s[g] = s_prev * cdec + new_states

    y = jnp.concatenate(y_parts, axis=1)
    zh = 0.5 * z_ref[:, x0:x0 + gw]
    y = y * (zh + zh * jnp.tanh(zh))
    ms = jnp.mean(y * y, axis=1, keepdims=True)
    o_ref[:, x0:x0 + gw] = (y * lax.rsqrt(ms + NORM_EPS)
                            * nw_ref[:, x0:x0 + gw]).astype(o_ref.dtype)


def _ssd_mixer(proj_a, dt_raw, conv_w, conv_b, dt_bias, a_log, d_exp, norm_w,
               *, batch, seq, d_ssm):
    L = SSM_CHUNK
    gs = SSD_GROUPS_PER_STEP
    gw = d_ssm // SSM_GROUPS
    xw = gs * gw
    sw = gs * SSM_STATE
    nchunk = seq // L
    t = batch * seq
    assert SSM_GROUPS % gs == 0
    xs0 = d_ssm // xw
    b0 = 2 * d_ssm // sw
    c0 = b0 + SSM_GROUPS // gs
    cwb0 = d_ssm // sw
    cwc0 = cwb0 + SSM_GROUPS // gs

    def rows(b, c, g):
        return b * nchunk + c

    in_specs = [
        pl.BlockSpec((L, xw), lambda b, c, g: (rows(b, c, g), g)),
        pl.BlockSpec((L, xw), lambda b, c, g: (rows(b, c, g), xs0 + g)),
        pl.BlockSpec((L, sw), lambda b, c, g: (rows(b, c, g), b0 + g)),
        pl.BlockSpec((L, sw), lambda b, c, g: (rows(b, c, g), c0 + g)),
        pl.BlockSpec((L, LANES), lambda b, c, g: (rows(b, c, g), 0)),
        pl.BlockSpec((CONV_WIDTH, xw), lambda b, c, g: (0, g)),
        pl.BlockSpec((CONV_WIDTH, sw), lambda b, c, g: (0, cwb0 + g)),
        pl.BlockSpec((CONV_WIDTH, sw), lambda b, c, g: (0, cwc0 + g)),
        pl.BlockSpec((1, xw), lambda b, c, g: (0, g)),
        pl.BlockSpec((1, sw), lambda b, c, g: (0, cwb0 + g)),
        pl.BlockSpec((1, sw), lambda b, c, g: (0, cwc0 + g)),
        pl.BlockSpec((1, LANES), lambda b, c, g: (0, 0)),
        pl.BlockSpec((1, LANES), lambda b, c, g: (0, 0)),
        pl.BlockSpec((1, xw), lambda b, c, g: (0, g)),
        pl.BlockSpec((1, xw), lambda b, c, g: (0, g)),
    ]
    return pl.pallas_call(
        _ssd_kernel,
        grid=(batch, nchunk, SSM_GROUPS // gs),
        in_specs=in_specs,
        out_specs=pl.BlockSpec((L, xw), lambda b, c, g: (rows(b, c, g), g)),
        out_shape=jax.ShapeDtypeStruct((t, d_ssm), BF16),
        scratch_shapes=[
            pltpu.VMEM((L, LANES), F32),
            pltpu.VMEM((L, LANES), F32),
            pltpu.VMEM((LANES, L), F32),
            pltpu.VMEM((SSM_GROUPS, SSM_STATE, gw), F32),
            pltpu.VMEM((SSM_GROUPS, SUBLANES, gw + 2 * SSM_STATE), F32),
            pltpu.VMEM((gs * (gw // LANES + 2), SUBLANES + L, LANES), F32),
        ],
        compiler_params=pltpu.CompilerParams(
            dimension_semantics=("arbitrary", "arbitrary", "arbitrary"),
            vmem_limit_bytes=32 * MIB),
        name="ssd_mixer",
    )(proj_a, proj_a, proj_a, proj_a, dt_raw,
      conv_w, conv_w, conv_w, conv_b, conv_b, conv_b,
      dt_bias, a_log, d_exp, norm_w)


def _lru_kernel(lx_ref, lg_ref, cw_ref, cb_ref, wa_ref, ba_ref, wx_ref, bx_ref,
                lam_ref, o_ref, tail_s, h_s):
    tb = pl.program_id(2)
    n_slab, ts, _ = lx_ref.shape
    n_group = ts // LRU_GROUP
    slabs_per_head = LRU_BLOCK // LANES
    S = SUBLANES
    vshape = (S, LANES)

    @pl.when(tb == 0)
    def _():
        tail_s[...] = jnp.zeros(tail_s.shape, F32)
        h_s[...] = jnp.zeros(h_s.shape, F32)

    sub = lax.broadcasted_iota(jnp.int32, vshape, 0)
    seg0 = sub == 0

    def seg_rows(ref, jj, row0):
        return ref[jj, pl.ds(row0, S, stride=S), :]

    u = []
    for jj in range(n_slab):
        lanes = slice(jj * LANES, (jj + 1) * LANES)
        taps = [jnp.broadcast_to(cw_ref[k:k + 1, lanes], vshape) for k in range(CONV_WIDTH)]
        bias = jnp.broadcast_to(cb_ref[:, lanes], vshape)
        u_slab = []
        for g in range(n_group):
            g0 = g * LRU_GROUP
            x_t = [seg_rows(lx_ref, jj, g0 + t) for t in range(S)]
            before = []
            for d in range(CONV_WIDTH - 1, 0, -1):
                if g == 0:
                    prev_row = jnp.broadcast_to(tail_s[jj, S - d:S - d + 1, :], vshape)
                    before.append(jnp.where(seg0, prev_row,
                                            pltpu.roll(x_t[S - d], 1, axis=0)))
                else:
                    before.append(seg_rows(lx_ref, jj, g0 - d))
            ext = before + x_t
            u_g = []
            for t in range(S):
                acc = ext[t + 3] * taps[3] + bias
                for k in range(CONV_WIDTH - 1):
                    acc = acc + ext[t + k] * taps[k]
                u_g.append(acc)
            u_slab.append(u_g)
        tail_s[jj] = lx_ref[jj, ts - S:ts, :]
        u.append(u_slab)

    pre_r, pre_i = [], []
    for h in range(n_slab // slabs_per_head):
        lhs = jnp.concatenate(
            [jnp.concatenate([u[h * slabs_per_head + q][g][t] for q in range(slabs_per_head)],
                             axis=1)
             for g in range(n_group) for t in range(S)], axis=0).astype(BF16)
        pre_r.append(jnp.dot(lhs, wa_ref[h], preferred_element_type=F32))
        pre_i.append(jnp.dot(lhs, wx_ref[h], preferred_element_type=F32))

    for jj in range(n_slab):
        lanes = slice(jj * LANES, (jj + 1) * LANES)
        h_idx, q = divmod(jj, slabs_per_head)
        qlanes = slice(q * LANES, (q + 1) * LANES)
        half_coef = jnp.broadcast_to(-0.5 * LRU_C * _softplus(-lam_ref[:, lanes]), vshape)
        b_r = jnp.broadcast_to(ba_ref[:, lanes], vshape)
        b_i = jnp.broadcast_to(bx_ref[:, lanes], vshape)
        carry = h_s[jj]
        for g in range(n_group):
            g0 = g * LRU_GROUP
            a_t, h_t = [], []
            for t in range(S):
                r0 = g0 + t * S
                t_r = jnp.tanh(pre_r[h_idx][r0:r0 + S, qlanes] + b_r)
                t_i = jnp.tanh(pre_i[h_idx][r0:r0 + S, qlanes] + b_i)
                log_a = half_coef * t_r + half_coef
                a = jnp.exp(log_a)
                th = jnp.tanh(log_a)
                q4 = (-0.5 * th) / (1.0 - th)
                root = jnp.where(q4 > 0.0, q4 * lax.rsqrt(q4), 0.0)
                bv = root * ((t_i + 1.0) * u[jj][g][t])
                if t == 0:
                    a_t.append(a)
                    h_t.append(bv)
                else:
                    a_t.append(a * a_t[-1])
                    h_t.append(a * h_t[-1] + bv)
            a_e, h_e = a_t[-1], h_t[-1]
            k = 1
            while k < S:
                keep = sub >= k
                a_sh = jnp.where(keep, pltpu.roll(a_e, k, axis=0), 1.0)
                h_sh = jnp.where(keep, pltpu.roll(h_e, k, axis=0), 0.0)
                h_e = a_e * h_sh + h_e
                a_e = a_e * a_sh
                k *= 2
            ends = h_e + a_e * carry
            seg_in = jnp.where(seg0, carry, pltpu.roll(ends, 1, axis=0))
            carry = jnp.broadcast_to(ends[S - 1:S, :], vshape)
            for t in range(S):
                hv = h_t[t] + a_t[t] * seg_in
                lh = 0.5 * seg_rows(lg_ref, jj, g0 + t)
                silu = lh + lh * jnp.tanh(lh)
                o_ref[jj, pl.ds(g0 + t, S, stride=S), :] = hv * silu
        h_s[jj] = carry


def _lru_mixer(proj_l, conv_w, conv_b, wa, ba, wx, bx, lam, *, batch, seq, d_lru, ts, cbw):
    t = batch * seq
    nt = seq // ts
    ncb = d_lru // cbw
    n_slab = cbw // LANES
    hb = cbw // LRU_BLOCK
    assert ts % LRU_GROUP == 0 and seq % ts == 0
    vec = pl.BlockSpec((1, cbw), lambda b, j, s: (0, j))
    return pl.pallas_call(
        _lru_kernel,
        grid=(batch, ncb, nt),
        in_specs=[
            pl.BlockSpec((n_slab, ts, LANES), lambda b, j, s: (j, b * nt + s, 0)),
            pl.BlockSpec((n_slab, ts, LANES), lambda b, j, s: (ncb + j, b * nt + s, 0)),
            pl.BlockSpec((CONV_WIDTH, cbw), lambda b, j, s: (0, j)),
            vec,
            pl.BlockSpec((hb, LRU_BLOCK, LRU_BLOCK), lambda b, j, s: (j, 0, 0)),
            vec,
            pl.BlockSpec((hb, LRU_BLOCK, LRU_BLOCK), lambda b, j, s: (j, 0, 0)),
            vec,
            vec,
        ],
        out_specs=pl.BlockSpec((n_slab, ts, LANES), lambda b, j, s: (j, b * nt + s, 0)),
        out_shape=jax.ShapeDtypeStruct((d_lru // LANES, t, LANES), F32),
        scratch_shapes=[pltpu.VMEM((n_slab, SUBLANES, LANES), F32),
                        pltpu.VMEM((n_slab, SUBLANES, LANES), F32)],
        compiler_params=pltpu.CompilerParams(
            dimension_semantics=("arbitrary", "arbitrary", "arbitrary"),
            vmem_limit_bytes=32 * MIB),
        name="lru_mixer",
    )(proj_l, proj_l, conv_w, conv_b, wa, ba, wx, bx, lam)


def _out_kernel(ssd_ref, lru_ref, w_ref, x_hbm, g_ref, b_ref, o_ref, xbuf, sem,
                *, nk_half, alpha):
    i = pl.program_id(0)
    k = pl.program_id(1)
    tm, d_model = o_ref.shape

    def x_copy():
        return pltpu.make_async_copy(x_hbm.at[pl.ds(i * tm, tm), :], xbuf, sem)

    def accumulate(lhs, first):
        for n in range(0, d_model, OUT_N_CHUNK):
            sl = slice(n, n + OUT_N_CHUNK)
            part = jnp.dot(lhs, w_ref[:, sl], preferred_element_type=F32)
            if first:
                o_ref[:, sl] = part
            else:
                o_ref[:, sl] += part

    @pl.when(k == 0)
    def _():
        x_copy().start()
        accumulate(ssd_ref[...], True)

    @pl.when(jnp.logical_and(k > 0, k < nk_half))
    def _():
        accumulate(ssd_ref[...], False)

    @pl.when(k >= nk_half)
    def _():
        lhs = jnp.concatenate([lru_ref[jj] for jj in range(lru_ref.shape[0])], axis=1)
        accumulate(lhs.astype(BF16), False)

    @pl.when(k == 2 * nk_half - 1)
    def _():
        x_copy().wait()

        def ln_rows(r, carry):
            rows = pl.ds(pl.multiple_of(r * LN_ROWS, LN_ROWS), LN_ROWS)
            res = o_ref[rows, :] + alpha * xbuf[rows, :]
            mu = jnp.mean(res, axis=1, keepdims=True)
            cen = res - mu
            var = jnp.mean(cen * cen, axis=1, keepdims=True)
            o_ref[rows, :] = cen * lax.rsqrt(var + NORM_EPS) * g_ref[...] + b_ref[...]
            return carry

        lax.fori_loop(0, tm // LN_ROWS, ln_rows, 0)


def _out_proj(ssd_out, lru_out, w_out, x2d, ln_g, ln_b, *, alpha, tm, tk):
    t, d_half = ssd_out.shape
    d_model = x2d.shape[1]
    nk_half = d_half // tk
    ks = tk // LANES
    vmem = (2 * (tm * tk * 2 + tm * tk * 4 + tk * d_model * 2 + tm * d_model * 4)
            + tm * d_model * 4 + 8 * MIB)
    return pl.pallas_call(
        functools.partial(_out_kernel, nk_half=nk_half, alpha=alpha),
        grid=(t // tm, 2 * nk_half),
        in_specs=[
            pl.BlockSpec((tm, tk), lambda i, k: (i, jnp.minimum(k, nk_half - 1))),
            pl.BlockSpec((ks, tm, LANES), lambda i, k: (jnp.maximum(k - nk_half, 0), i, 0)),
            pl.BlockSpec((tk, d_model), lambda i, k: (k, 0)),
            pl.BlockSpec(memory_space=pl.ANY),
            pl.BlockSpec((1, d_model), lambda i, k: (0, 0)),
            pl.BlockSpec((1, d_model), lambda i, k: (0, 0)),
        ],
        out_specs=pl.BlockSpec((tm, d_model), lambda i, k: (i, 0)),
        out_shape=jax.ShapeDtypeStruct((t, d_model), F32),
        scratch_shapes=[pltpu.VMEM((tm, d_model), F32), pltpu.SemaphoreType.DMA(())],
        compiler_params=pltpu.CompilerParams(
            dimension_semantics=("arbitrary", "arbitrary"),
            vmem_limit_bytes=vmem),
        name="out_proj_ln",
    )(ssd_out, lru_out, w_out, x2d, ln_g, ln_b)


def _layer(x2d, layer, w_in, ssd_conv_w, ssd_conv_b, ssd_dt_bias, ssd_a_log, ssd_d,
           ssd_norm_w, lru_conv_w, lru_conv_b, lru_wa, lru_ba, lru_wx, lru_bx, lru_lambda,
           w_out, ln_g, ln_b, *, batch, seq, alpha):
    t, d_model = x2d.shape
    d_ssm = d_model
    d_lru = d_model
    heads = d_ssm // SSM_HEADDIM
    d_xbc = d_ssm + 2 * SSM_GROUPS * SSM_STATE
    n_a = d_ssm + d_xbc
    assert heads <= LANES and seq % SSM_CHUNK == 0

    xb = x2d.astype(BF16)
    w_t = jnp.swapaxes(w_in, 1, 2)

    tm = min(1024, t)
    proj_a = _matmul(xb, w_t, layer=layer, row0=0, n_cols=n_a, tm=tm, tn=1024)
    dt_raw = _matmul(xb, w_t, layer=layer, row0=n_a, n_cols=LANES, tm=tm, tn=LANES)
    proj_l = _matmul(xb, w_t, layer=layer, row0=n_a + heads, n_cols=2 * d_lru, tm=tm,
                     tn=1024, slab_out=True)

    pad_h = (0, LANES - heads)
    ssd_out = _ssd_mixer(
        proj_a, dt_raw, ssd_conv_w, ssd_conv_b.reshape(1, d_xbc),
        jnp.pad(ssd_dt_bias, pad_h).reshape(1, LANES),
        jnp.pad(ssd_a_log, pad_h).reshape(1, LANES),
        jnp.repeat(ssd_d, SSM_HEADDIM).reshape(1, d_ssm),
        ssd_norm_w.reshape(1, d_ssm),
        batch=batch, seq=seq, d_ssm=d_ssm)

    lru_out = _lru_mixer(
        proj_l, lru_conv_w, lru_conv_b.reshape(1, d_lru),
        (0.5 * lru_wa).astype(BF16), 0.5 * lru_ba.reshape(1, d_lru),
        (0.5 * lru_wx).astype(BF16), 0.5 * lru_bx.reshape(1, d_lru),
        lru_lambda.reshape(1, d_lru),
        batch=batch, seq=seq, d_lru=d_lru, ts=min(512, seq), cbw=512)

    return _out_proj(ssd_out, lru_out, w_out.astype(BF16), x2d,
                     ln_g.reshape(1, d_model), ln_b.reshape(1, d_model),
                     alpha=alpha, tm=min(512, t), tk=1024)


def kernel(x, w_in, ssd_conv_w, ssd_conv_b, ssd_dt_bias, ssd_a_log, ssd_d, ssd_norm_w,
           lru_conv_w, lru_conv_b, lru_wa, lru_ba, lru_wx, lru_bx, lru_lambda,
           w_out, ln_g, ln_b):
    batch, seq, d_model = x.shape
    depth = w_in.shape[0]
    alpha = (2.0 * depth) ** 0.25
    h = x.reshape(batch * seq, d_model)
    for layer in range(depth):
        h = _layer(h, layer, w_in, ssd_conv_w[layer], ssd_conv_b[layer],
                   ssd_dt_bias[layer], ssd_a_log[layer], ssd_d[layer], ssd_norm_w[layer],
                   lru_conv_w[layer], lru_conv_b[layer], lru_wa[layer], lru_ba[layer],
                   lru_wx[layer], lru_bx[layer], lru_lambda[layer],
                   w_out[layer], ln_g[layer], ln_b[layer],
                   batch=batch, seq=seq, alpha=alpha)
    return h.reshape(batch, seq, d_model)
```

```python
import functools

import jax
import jax.numpy as jnp
from jax import lax
from jax.experimental import pallas as pl
from jax.experimental.pallas import tpu as pltpu

F32 = jnp.float32
BF16 = jnp.bfloat16

SSM_HEADDIM = 64
SSM_GROUPS = 8
SSM_STATE = 128
SSM_CHUNK = 128
LRU_BLOCK = 256
LRU_C = 8.0
CONV_WIDTH = 4
NORM_EPS = 1e-5
LOG2E = 1.4426950408889634

LANES = 128
SUBLANES = 8
MIB = 1024 * 1024
OUT_N_CHUNK = 512
LN_ROWS = 64
LRU_GROUP = SUBLANES * SUBLANES
SSD_GROUPS_PER_STEP = 8


def _softplus(v):
    return jnp.maximum(v, 0.0) + jnp.log1p(jnp.exp(-jnp.abs(v)))


def _mm_kernel(x_ref, w_ref, o_ref, wb_s, *, slab_out):
    @pl.when(pl.program_id(1) == 0)
    def _():
        for r in range(0, w_ref.shape[0], LANES):
            wb_s[:, r:r + LANES] = w_ref[r:r + LANES, :].T.astype(BF16)

    res = jnp.dot(x_ref[...], wb_s[...], preferred_element_type=F32)
    if slab_out:
        for jj in range(o_ref.shape[0]):
            o_ref[jj] = res[:, jj * LANES:(jj + 1) * LANES]
    else:
        o_ref[...] = res


def _matmul(x, w_t, *, layer, row0, n_cols, tm, tn, slab_out=False):
    m, k = x.shape
    assert row0 % SUBLANES == 0 and n_cols % tn == 0 and m % tm == 0 and tn % LANES == 0
    row0 = layer * w_t.shape[1] + row0
    w_spec = pl.BlockSpec((pl.Element(tn), pl.Element(k)),
                          lambda j, i: (pl.multiple_of(row0 + j * tn, SUBLANES), 0))
    if slab_out:
        out_spec = pl.BlockSpec((tn // LANES, tm, LANES), lambda j, i: (j, i, 0))
        out_shape = jax.ShapeDtypeStruct((n_cols // LANES, m, LANES), F32)
    else:
        out_spec = pl.BlockSpec((tm, tn), lambda j, i: (i, j))
        out_shape = jax.ShapeDtypeStruct((m, n_cols), F32)
    vmem = (2 * (tm * k * 2 + tn * k * 4 + tm * tn * 4) + tn * k * 2 + tm * tn * 4
            + 6 * MIB)
    return pl.pallas_call(
        functools.partial(_mm_kernel, slab_out=slab_out),
        grid=(n_cols // tn, m // tm),
        in_specs=[pl.BlockSpec((tm, k), lambda j, i: (i, 0)), w_spec],
        out_specs=out_spec,
        out_shape=out_shape,
        scratch_shapes=[pltpu.VMEM((k, tn), BF16)],
        compiler_params=pltpu.CompilerParams(
            dimension_semantics=("arbitrary", "arbitrary"),
            vmem_limit_bytes=vmem),
        name="in_proj",
    )(x, w_t.reshape(-1, k))


def _xcast_dt_kernel(x_ref, w_ref, xb_ref, dt_ref, wb_s):
    @pl.when(pl.program_id(0) == 0)
    def _():
        wb_s[...] = w_ref[...].T.astype(BF16)

    xb = x_ref[...].astype(BF16)
    xb_ref[...] = xb
    dt_ref[...] = jnp.dot(xb, wb_s[...], preferred_element_type=F32)


def _xcast_dt(x2d, w_t, *, layer, row0, tm):
    m, k = x2d.shape
    assert row0 % SUBLANES == 0 and m % tm == 0
    row0 = layer * w_t.shape[1] + row0
    vmem = 2 * (tm * k * 4 + tm * k * 2 + LANES * k * 4 + tm * LANES * 4) + 8 * MIB
    return pl.pallas_call(
        _xcast_dt_kernel,
        grid=(m // tm,),
        in_specs=[pl.BlockSpec((tm, k), lambda i: (i, 0)),
                  pl.BlockSpec((pl.Element(LANES), pl.Element(k)), lambda i: (row0, 0))],
        out_specs=[pl.BlockSpec((tm, k), lambda i: (i, 0)),
                   pl.BlockSpec((tm, LANES), lambda i: (i, 0))],
        out_shape=[jax.ShapeDtypeStruct((m, k), BF16),
                   jax.ShapeDtypeStruct((m, LANES), F32)],
        scratch_shapes=[pltpu.VMEM((k, LANES), BF16)],
        compiler_params=pltpu.CompilerParams(
            dimension_semantics=("arbitrary",), vmem_limit_bytes=vmem),
        name="xcast_dt",
    )(x2d, w_t.reshape(-1, k))


def _ssd_kernel(z_ref, xs_ref, b_ref, c_ref, dt_ref,
                cwx_ref, cwb_ref, cwc_ref, cbx_ref, cbb_ref, cbc_ref,
                dtb_ref, alog_ref, d_ref, nw_ref,
                o_ref,
                dt_s, acs_s, acst_s, state_s, tail_s, ext_s):
    c = pl.program_id(1)
    gstep = pl.program_id(2)
    L = SSM_CHUNK
    gs = SSD_GROUPS_PER_STEP
    gw = xs_ref.shape[1] // gs

    row_id = lax.broadcasted_iota(jnp.int32, (L, LANES), 0)

    @pl.when(gstep == 0)
    def _():
        dt = _softplus(dt_ref[...] + dtb_ref[...])
        a_dt = dt * (-jnp.exp(alog_ref[...]))
        acs = a_dt
        k = 1
        while k < L:
            acs = acs + jnp.where(row_id >= k, pltpu.roll(acs, k, axis=0), 0.0)
            k *= 2
        acs2 = acs * LOG2E
        dt_s[...] = dt
        acs_s[...] = acs2
        acst_s[...] = acs2.T

    for gg in range(gs):
        _ssd_group(gg, gstep * gs + gg, c, gw,
                   z_ref, xs_ref, b_ref, c_ref, cwx_ref, cwb_ref, cwc_ref,
                   cbx_ref, cbb_ref, cbc_ref, d_ref, nw_ref, o_ref,
                   dt_s, acs_s, acst_s, state_s, tail_s, ext_s)


def _ssd_group(gg, g, c, gw, z_ref, xs_ref, b_ref, c_ref, cwx_ref, cwb_ref, cwc_ref,
               cbx_ref, cbb_ref, cbc_ref, d_ref, nw_ref, o_ref,
               dt_s, acs_s, acst_s, state_s, tail_s, ext_s):
    L = SSM_CHUNK
    heads_per_group = gw // SSM_HEADDIM
    pairs = gw // LANES
    n_xs = gw // LANES
    x0 = gg * gw
    bc0 = gg * SSM_STATE
    row_id = lax.broadcasted_iota(jnp.int32, (L, LANES), 0)
    lane_id = lax.broadcasted_iota(jnp.int32, (L, LANES), 1)

    @pl.when(c == 0)
    def _():
        state_s[g] = jnp.zeros(state_s.shape[1:], F32)
        tail_s[g] = jnp.zeros(tail_s.shape[1:], F32)

    acts = []
    for s in range(n_xs + 2):
        if s < n_xs:
            lanes = slice(x0 + s * LANES, x0 + (s + 1) * LANES)
            in_ref, w_ref, bias_ref = xs_ref, cwx_ref, cbx_ref
        else:
            lanes = slice(bc0, bc0 + LANES)
            in_ref, w_ref, bias_ref = ((b_ref, cwb_ref, cbb_ref) if s == n_xs
                                       else (c_ref, cwc_ref, cbc_ref))
        raw = in_ref[:, lanes]
        slab = gg * (n_xs + 2) + s
        ext_s[slab, 0:SUBLANES, :] = tail_s[g, :, s * LANES:(s + 1) * LANES]
        ext_s[slab, SUBLANES:SUBLANES + L, :] = raw
        tail_s[g, :, s * LANES:(s + 1) * LANES] = raw[L - SUBLANES:, :]
        acc = raw * w_ref[CONV_WIDTH - 1:CONV_WIDTH, lanes] + bias_ref[:, lanes]
        for k in range(1, CONV_WIDTH):
            acc = acc + (ext_s[slab, pl.ds(SUBLANES - k, L), :]
                         * w_ref[CONV_WIDTH - 1 - k:CONV_WIDTH - k, lanes])
        half = 0.5 * acc
        acts.append(half + half * jnp.tanh(half))
    xs = jnp.concatenate(acts[:n_xs], axis=1)
    bm = acts[n_xs].astype(BF16)
    cm = acts[n_xs + 1].astype(BF16)

    cb_mat = lax.dot_general(cm, bm, (((1,), (1,)), ((), ())),
                             preferred_element_type=F32)
    s_prev = state_s[g]
    y_off = jnp.dot(cm, s_prev.astype(BF16), preferred_element_type=F32)

    dt_all = dt_s[...]
    acs_all = acs_s[...]
    causal = row_id >= lane_id
    low = lane_id < SSM_HEADDIM

    def head_col(arr, idx):
        return jnp.sum(jnp.where(lane_id == idx, arr, 0.0), axis=1, keepdims=True)

    def head_pair(arr, idx0):
        return jnp.where(low, head_col(arr, idx0), head_col(arr, idx0 + 1))

    y_parts, xdec_parts, cdec_parts = [], [], []
    for j in range(pairs):
        i0 = g * heads_per_group + 2 * j
        i1 = i0 + 1
        col0 = head_col(acs_all, i0)
        col1 = head_col(acs_all, i1)
        row0 = acst_s[pl.ds(i0, 1), :]
        row1 = acst_s[pl.ds(i1, 1), :]
        l0 = jnp.where(causal, jnp.exp2(col0 - row0), 0.0)
        l1 = jnp.where(causal, jnp.exp2(col1 - row1), 0.0)
        m_pair = jnp.concatenate([cb_mat * l0, cb_mat * l1], axis=1).astype(BF16)

        dt_pair = head_pair(dt_all, i0)
        xs_pair = xs[:, j * LANES:(j + 1) * LANES]
        xt = xs_pair * dt_pair
        xt_b = xt.astype(BF16)
        zero = jnp.zeros_like(xt_b)
        rhs = jnp.concatenate([jnp.where(low, xt_b, zero),
                               jnp.where(low, zero, xt_b)], axis=0)
        y_diag = jnp.dot(m_pair, rhs, preferred_element_type=F32)

        scale = jnp.where(low, jnp.exp2(col0), jnp.exp2(col1))
        last0 = col0[L - 1:L, :]
        last1 = col1[L - 1:L, :]
        dec = jnp.where(low, jnp.exp2(last0 - col0), jnp.exp2(last1 - col1))
        y_parts.append(y_diag + y_off[:, j * LANES:(j + 1) * LANES] * scale
                       + d_ref[:, x0 + j * LANES:x0 + (j + 1) * LANES] * xs_pair)
        xdec_parts.append((xt * dec).astype(BF16))
        cdec_parts.append(scale[L - 1:L, :])

    xdec = jnp.concatenate(xdec_parts, axis=1)
    cdec = jnp.concatenate(cdec_parts, axis=1)
    new_states = lax.dot_general(bm, xdec, (((0,), (0,)), ((), ())),
                                 preferred_element_type=F32)
    state_s[g] = s_prev * cdec + new_states

    y = jnp.concatenate(y_parts, axis=1)
    zh = 0.5 * z_ref[:, x0:x0 + gw]
    y = y * (zh + zh * jnp.tanh(zh))
    ms = jnp.mean(y * y, axis=1, keepdims=True)
    o_ref[:, x0:x0 + gw] = (y * lax.rsqrt(ms + NORM_EPS)
                            * nw_ref[:, x0:x0 + gw]).astype(o_ref.dtype)


def _ssd_mixer(proj_a, dt_raw, conv_w, conv_b, dt_bias, a_log, d_exp, norm_w,
               *, batch, seq, d_ssm):
    L = SSM_CHUNK
    gs = SSD_GROUPS_PER_STEP
    gw = d_ssm // SSM_GROUPS
    xw = gs * gw
    sw = gs * SSM_STATE
    nchunk = seq // L
    t = batch * seq
    assert SSM_GROUPS % gs == 0
    xs0 = d_ssm // xw
    b0 = 2 * d_ssm // sw
    c0 = b0 + SSM_GROUPS // gs
    cwb0 = d_ssm // sw
    cwc0 = cwb0 + SSM_GROUPS // gs

    def rows(b, c, g):
        return b * nchunk + c

    in_specs = [
        pl.BlockSpec((L, xw), lambda b, c, g: (rows(b, c, g), g)),
        pl.BlockSpec((L, xw), lambda b, c, g: (rows(b, c, g), xs0 + g)),
        pl.BlockSpec((L, sw), lambda b, c, g: (rows(b, c, g), b0 + g)),
        pl.BlockSpec((L, sw), lambda b, c, g: (rows(b, c, g), c0 + g)),
        pl.BlockSpec((L, LANES), lambda b, c, g: (rows(b, c, g), 0)),
        pl.BlockSpec((CONV_WIDTH, xw), lambda b, c, g: (0, g)),
        pl.BlockSpec((CONV_WIDTH, sw), lambda b, c, g: (0, cwb0 + g)),
        pl.BlockSpec((CONV_WIDTH, sw), lambda b, c, g: (0, cwc0 + g)),
        pl.BlockSpec((1, xw), lambda b, c, g: (0, g)),
        pl.BlockSpec((1, sw), lambda b, c, g: (0, cwb0 + g)),
        pl.BlockSpec((1, sw), lambda b, c, g: (0, cwc0 + g)),
        pl.BlockSpec((1, LANES), lambda b, c, g: (0, 0)),
        pl.BlockSpec((1, LANES), lambda b, c, g: (0, 0)),
        pl.BlockSpec((1, xw), lambda b, c, g: (0, g)),
        pl.BlockSpec((1, xw), lambda b, c, g: (0, g)),
    ]
    return pl.pallas_call(
        _ssd_kernel,
        grid=(batch, nchunk, SSM_GROUPS // gs),
        in_specs=in_specs,
        out_specs=pl.BlockSpec((L, xw), lambda b, c, g: (rows(b, c, g), g)),
        out_shape=jax.ShapeDtypeStruct((t, d_ssm), BF16),
        scratch_shapes=[
            pltpu.VMEM((L, LANES), F32),
            pltpu.VMEM((L, LANES), F32),
            pltpu.VMEM((LANES, L), F32),
            pltpu.VMEM((SSM_GROUPS, SSM_STATE, gw), F32),
            pltpu.VMEM((SSM_GROUPS, SUBLANES, gw + 2 * SSM_STATE), F32),
            pltpu.VMEM((gs * (gw // LANES + 2), SUBLANES + L, LANES), F32),
        ],
        compiler_params=pltpu.CompilerParams(
            dimension_semantics=("arbitrary", "arbitrary", "arbitrary"),
            vmem_limit_bytes=32 * MIB),
        name="ssd_mixer",
    )(proj_a, proj_a, proj_a, proj_a, dt_raw,
      conv_w, conv_w, conv_w, conv_b, conv_b, conv_b,
      dt_bias, a_log, d_exp, norm_w)


def _lru_kernel(lx_ref, lg_ref, cw_ref, cb_ref, wa_ref, ba_ref, wx_ref, bx_ref,
                lam_ref, o_ref, tail_s, h_s):
    tb = pl.program_id(2)
    n_slab, ts, _ = lx_ref.shape
    n_group = ts // LRU_GROUP
    slabs_per_head = LRU_BLOCK // LANES
    S = SUBLANES
    vshape = (S, LANES)

    @pl.when(tb == 0)
    def _():
        tail_s[...] = jnp.zeros(tail_s.shape, F32)
        h_s[...] = jnp.zeros(h_s.shape, F32)

    sub = lax.broadcasted_iota(jnp.int32, vshape, 0)
    seg0 = sub == 0

    def seg_rows(ref, jj, row0):
        return ref[jj, pl.ds(row0, S, stride=S), :]

    u = []
    for jj in range(n_slab):
        lanes = slice(jj * LANES, (jj + 1) * LANES)
        taps = [jnp.broadcast_to(cw_ref[k:k + 1, lanes], vshape) for k in range(CONV_WIDTH)]
        bias = jnp.broadcast_to(cb_ref[:, lanes], vshape)
        u_slab = []
        for g in range(n_group):
            g0 = g * LRU_GROUP
            x_t = [seg_rows(lx_ref, jj, g0 + t) for t in range(S)]
            before = []
            for d in range(CONV_WIDTH - 1, 0, -1):
                if g == 0:
                    prev_row = jnp.broadcast_to(tail_s[jj, S - d:S - d + 1, :], vshape)
                    before.append(jnp.where(seg0, prev_row,
                                            pltpu.roll(x_t[S - d], 1, axis=0)))
                else:
                    before.append(seg_rows(lx_ref, jj, g0 - d))
            ext = before + x_t
            u_g = []
            for t in range(S):
                acc = ext[t + 3] * taps[3] + bias
                for k in range(CONV_WIDTH - 1):
                    acc = acc + ext[t + k] * taps[k]
                u_g.append(acc)
            u_slab.append(u_g)
        tail_s[jj] = lx_ref[jj, ts - S:ts, :]
        u.append(u_slab)

    pre_r, pre_i = [], []
    for h in range(n_slab // slabs_per_head):
        lhs = jnp.concatenate(
            [jnp.concatenate([u[h * slabs_per_head + q][g][t] for q in range(slabs_per_head)],
                             axis=1)
             for g in range(n_group) for t in range(S)], axis=0).astype(BF16)
        pre_r.append(jnp.dot(lhs, wa_ref[h], preferred_element_type=F32))
        pre_i.append(jnp.dot(lhs, wx_ref[h], preferred_element_type=F32))

    for jj in range(n_slab):
        lanes = slice(jj * LANES, (jj + 1) * LANES)
        h_idx, q = divmod(jj, slabs_per_head)
        qlanes = slice(q * LANES, (q + 1) * LANES)
        half_coef = jnp.broadcast_to(-0.5 * LRU_C * _softplus(-lam_ref[:, lanes]), vshape)
        b_r = jnp.broadcast_to(ba_ref[:, lanes], vshape)
        b_i = jnp.broadcast_to(bx_ref[:, lanes], vshape)
        carry = h_s[jj]
        for g in range(n_group):
            g0 = g * LRU_GROUP
            a_t, h_t = [], []
            for t in range(S):
                r0 = g0 + t * S
                t_r = jnp.tanh(pre_r[h_idx][r0:r0 + S, qlanes] + b_r)
                t_i = jnp.tanh(pre_i[h_idx][r0:r0 + S, qlanes] + b_i)
                log_a = half_coef * t_r + half_coef
                a = jnp.exp(log_a)
                th = jnp.tanh(log_a)
                q4 = (-0.5 * th) / (1.0 - th)
                root = jnp.where(q4 > 0.0, q4 * lax.rsqrt(q4), 0.0)
                bv = root * ((t_i + 1.0) * u[jj][g][t])
                if t == 0:
                    a_t.append(a)
                    h_t.append(bv)
                else:
                    a_t.append(a * a_t[-1])
                    h_t.append(a * h_t[-1] + bv)
            a_e, h_e = a_t[-1], h_t[-1]
            k = 1
            while k < S:
                keep = sub >= k
                a_sh = jnp.where(keep, pltpu.roll(a_e, k, axis=0), 1.0)
                h_sh = jnp.where(keep, pltpu.roll(h_e, k, axis=0), 0.0)
                h_e = a_e * h_sh + h_e
                a_e = a_e * a_sh
                k *= 2
            ends = h_e + a_e * carry
            seg_in = jnp.where(seg0, carry, pltpu.roll(ends, 1, axis=0))
            carry = jnp.broadcast_to(ends[S - 1:S, :], vshape)
            for t in range(S):
                hv = h_t[t] + a_t[t] * seg_in
                lh = 0.5 * seg_rows(lg_ref, jj, g0 + t)
                silu = lh + lh * jnp.tanh(lh)
                o_ref[jj, pl.ds(g0 + t, S, stride=S), :] = hv * silu
        h_s[jj] = carry


def _lru_mixer(proj_l, conv_w, conv_b, wa, ba, wx, bx, lam, *, batch, seq, d_lru, ts, cbw):
    t = batch * seq
    nt = seq // ts
    ncb = d_lru // cbw
    n_slab = cbw // LANES
    hb = cbw // LRU_BLOCK
    assert ts % LRU_GROUP == 0 and seq % ts == 0
    vec = pl.BlockSpec((1, cbw), lambda b, j, s: (0, j))
    return pl.pallas_call(
        _lru_kernel,
        grid=(batch, ncb, nt),
        in_specs=[
            pl.BlockSpec((n_slab, ts, LANES), lambda b, j, s: (j, b * nt + s, 0)),
            pl.BlockSpec((n_slab, ts, LANES), lambda b, j, s: (ncb + j, b * nt + s, 0)),
            pl.BlockSpec((CONV_WIDTH, cbw), lambda b, j, s: (0, j)),
            vec,
            pl.BlockSpec((hb, LRU_BLOCK, LRU_BLOCK), lambda b, j, s: (j, 0, 0)),
            vec,
            pl.BlockSpec((hb, LRU_BLOCK, LRU_BLOCK), lambda b, j, s: (j, 0, 0)),
            vec,
            vec,
        ],
        out_specs=pl.BlockSpec((n_slab, ts, LANES), lambda b, j, s: (j, b * nt + s, 0)),
        out_shape=jax.ShapeDtypeStruct((d_lru // LANES, t, LANES), F32),
        scratch_shapes=[pltpu.VMEM((n_slab, SUBLANES, LANES), F32),
                        pltpu.VMEM((n_slab, SUBLANES, LANES), F32)],
        compiler_params=pltpu.CompilerParams(
            dimension_semantics=("arbitrary", "arbitrary", "arbitrary"),
            vmem_limit_bytes=32 * MIB),
        name="lru_mixer",
    )(proj_l, proj_l, conv_w, conv_b, wa, ba, wx, bx, lam)


def _out_kernel(ssd_ref, lru_ref, w_ref, x_hbm, g_ref, b_ref, o_ref, xbuf, sem,
                *, nk_half, alpha):
    i = pl.program_id(0)
    k = pl.program_id(1)
    tm, d_model = o_ref.shape

    def x_copy():
        return pltpu.make_async_copy(x_hbm.at[pl.ds(i * tm, tm), :], xbuf, sem)

    def accumulate(lhs, first):
        for n in range(0, d_model, OUT_N_CHUNK):
            sl = slice(n, n + OUT_N_CHUNK)
            part = jnp.dot(lhs, w_ref[:, sl], preferred_element_type=F32)
            if first:
                o_ref[:, sl] = part
            else:
                o_ref[:, sl] += part

    @pl.when(k == 0)
    def _():
        x_copy().start()
        accumulate(ssd_ref[...], True)

    @pl.when(jnp.logical_and(k > 0, k < nk_half))
    def _():
        accumulate(ssd_ref[...], False)

    @pl.when(k >= nk_half)
    def _():
        lhs = jnp.concatenate([lru_ref[jj] for jj in range(lru_ref.shape[0])], axis=1)
        accumulate(lhs.astype(BF16), False)

    @pl.when(k == 2 * nk_half - 1)
    def _():
        x_copy().wait()

        def ln_rows(r, carry):
            rows = pl.ds(pl.multiple_of(r * LN_ROWS, LN_ROWS), LN_ROWS)
            res = o_ref[rows, :] + alpha * xbuf[rows, :]
            mu = jnp.mean(res, axis=1, keepdims=True)
            cen = res - mu
            var = jnp.mean(cen * cen, axis=1, keepdims=True)
            o_ref[rows, :] = cen * lax.rsqrt(var + NORM_EPS) * g_ref[...] + b_ref[...]
            return carry

        lax.fori_loop(0, tm // LN_ROWS, ln_rows, 0)


def _out_proj(ssd_out, lru_out, w_out, x2d, ln_g, ln_b, *, alpha, tm, tk):
    t, d_half = ssd_out.shape
    d_model = x2d.shape[1]
    nk_half = d_half // tk
    ks = tk // LANES
    vmem = (2 * (tm * tk * 2 + tm * tk * 4 + tk * d_model * 2 + tm * d_model * 4)
            + tm * d_model * 4 + 8 * MIB)
    return pl.pallas_call(
        functools.partial(_out_kernel, nk_half=nk_half, alpha=alpha),
        grid=(t // tm, 2 * nk_half),
        in_specs=[
            pl.BlockSpec((tm, tk), lambda i, k: (i, jnp.minimum(k, nk_half - 1))),
            pl.BlockSpec((ks, tm, LANES), lambda i, k: (jnp.maximum(k - nk_half, 0), i, 0)),
            pl.BlockSpec((tk, d_model), lambda i, k: (k, 0)),
            pl.BlockSpec(memory_space=pl.ANY),
            pl.BlockSpec((1, d_model), lambda i, k: (0, 0)),
            pl.BlockSpec((1, d_model), lambda i, k: (0, 0)),
        ],
        out_specs=pl.BlockSpec((tm, d_model), lambda i, k: (i, 0)),
        out_shape=jax.ShapeDtypeStruct((t, d_model), F32),
        scratch_shapes=[pltpu.VMEM((tm, d_model), F32), pltpu.SemaphoreType.DMA(())],
        compiler_params=pltpu.CompilerParams(
            dimension_semantics=("arbitrary", "arbitrary"),
            vmem_limit_bytes=vmem),
        name="out_proj_ln",
    )(ssd_out, lru_out, w_out, x2d, ln_g, ln_b)


def _layer(x2d, layer, w_in, ssd_conv_w, ssd_conv_b, ssd_dt_bias, ssd_a_log, ssd_d,
           ssd_norm_w, lru_conv_w, lru_conv_b, lru_wa, lru_ba, lru_wx, lru_bx, lru_lambda,
           w_out, ln_g, ln_b, *, batch, seq, alpha):
    t, d_model = x2d.shape
    d_ssm = d_model
    d_lru = d_model
    heads = d_ssm // SSM_HEADDIM
    d_xbc = d_ssm + 2 * SSM_GROUPS * SSM_STATE
    n_a = d_ssm + d_xbc
    assert heads <= LANES and seq % SSM_CHUNK == 0

    w_t = jnp.swapaxes(w_in, 1, 2)

    tm = min(1024, t)
    xb, dt_raw = _xcast_dt(x2d, w_t, layer=layer, row0=n_a, tm=min(512, t))
    proj_a = _matmul(xb, w_t, layer=layer, row0=0, n_cols=n_a, tm=tm, tn=512)
    proj_l = _matmul(xb, w_t, layer=layer, row0=n_a + heads, n_cols=2 * d_lru, tm=tm,
                     tn=512, slab_out=True)

    pad_h = (0, LANES - heads)
    ssd_out = _ssd_mixer(
        proj_a, dt_raw, ssd_conv_w, ssd_conv_b.reshape(1, d_xbc),
        jnp.pad(ssd_dt_bias, pad_h).reshape(1, LANES),
        jnp.pad(ssd_a_log, pad_h).reshape(1, LANES),
        jnp.repeat(ssd_d, SSM_HEADDIM).reshape(1, d_ssm),
        ssd_norm_w.reshape(1, d_ssm),
        batch=batch, seq=seq, d_ssm=d_ssm)

    lru_out = _lru_mixer(
        proj_l, lru_conv_w, lru_conv_b.reshape(1, d_lru),
        (0.5 * lru_wa).astype(BF16), 0.5 * lru_ba.reshape(1, d_lru),
        (0.5 * lru_wx).astype(BF16), 0.5 * lru_bx.reshape(1, d_lru),
        lru_lambda.reshape(1, d_lru),
        batch=batch, seq=seq, d_lru=d_lru, ts=min(512, seq), cbw=512)

    return _out_proj(ssd_out, lru_out, w_out.astype(BF16), x2d,
                     ln_g.reshape(1, d_model), ln_b.reshape(1, d_model),
                     alpha=alpha, tm=min(512, t), tk=1024)


def kernel(x, w_in, ssd_conv_w, ssd_conv_b, ssd_dt_bias, ssd_a_log, ssd_d, ssd_norm_w,
           lru_conv_w, lru_conv_b, lru_wa, lru_ba, lru_wx, lru_bx, lru_lambda,
           w_out, ln_g, ln_b):
    batch, seq, d_model = x.shape
    depth = w_in.shape[0]
    alpha = (2.0 * depth) ** 0.25
    h = x.reshape(batch * seq, d_model)
    for layer in range(depth):
        h = _layer(h, layer, w_in, ssd_conv_w[layer], ssd_conv_b[layer],
                   ssd_dt_bias[layer], ssd_a_log[layer], ssd_d[layer], ssd_norm_w[layer],
                   lru_conv_w[layer], lru_conv_b[layer], lru_wa[layer], lru_ba[layer],
                   lru_wx[layer], lru_bx[layer], lru_lambda[layer],
                   w_out[layer], ln_g[layer], ln_b[layer],
                   batch=batch, seq=seq, alpha=alpha)
    return h.reshape(batch, seq, d_model)
```

```python
import functools

import jax
import jax.numpy as jnp
from jax import lax
from jax.experimental import pallas as pl
from jax.experimental.pallas import tpu as pltpu

F32 = jnp.float32
BF16 = jnp.bfloat16

SSM_HEADDIM = 64
SSM_GROUPS = 8
SSM_STATE = 128
SSM_CHUNK = 128
LRU_BLOCK = 256
LRU_C = 8.0
CONV_WIDTH = 4
NORM_EPS = 1e-5
LOG2E = 1.4426950408889634

LANES = 128
SUBLANES = 8
MIB = 1024 * 1024
OUT_N_CHUNK = 512
LN_ROWS = 64
OUT_X_SLOTS = 4
LRU_GROUP = SUBLANES * SUBLANES
SSD_GROUPS_PER_STEP = 8

def _softplus(v):
    return jnp.maximum(v, 0.0) + jnp.log1p(jnp.exp(-jnp.abs(v)))


def _mm_kernel(x_ref, w_ref, o_ref, wb_s, *, slab_out):
    @pl.when(pl.program_id(1) == 0)
    def _():
        for r in range(0, w_ref.shape[0], LANES):
            wb_s[:, r:r + LANES] = w_ref[r:r + LANES, :].T.astype(BF16)

    res = jnp.dot(x_ref[...], wb_s[...], preferred_element_type=F32)
    if slab_out:
        for jj in range(o_ref.shape[0]):
            o_ref[jj] = res[:, jj * LANES:(jj + 1) * LANES]
    else:
        o_ref[...] = res


def _matmul(x, w_t, *, layer, row0, n_cols, tm, tn, slab_out=False):
    m, k = x.shape
    assert row0 % SUBLANES == 0 and n_cols % tn == 0 and m % tm == 0 and tn % LANES == 0
    row0 = layer * w_t.shape[1] + row0
    w_spec = pl.BlockSpec((pl.Element(tn), pl.Element(k)),
                          lambda j, i: (pl.multiple_of(row0 + j * tn, SUBLANES), 0))
    if slab_out:
        out_spec = pl.BlockSpec((tn // LANES, tm, LANES), lambda j, i: (j, i, 0))
        out_shape = jax.ShapeDtypeStruct((n_cols // LANES, m, LANES), F32)
    else:
        out_spec = pl.BlockSpec((tm, tn), lambda j, i: (i, j))
        out_shape = jax.ShapeDtypeStruct((m, n_cols), F32)
    vmem = (2 * (tm * k * 2 + tn * k * 4 + tm * tn * 4) + tn * k * 2 + tm * tn * 4
            + 6 * MIB)
    return pl.pallas_call(
        functools.partial(_mm_kernel, slab_out=slab_out),
        grid=(n_cols // tn, m // tm),
        in_specs=[pl.BlockSpec((tm, k), lambda j, i: (i, 0)), w_spec],
        out_specs=out_spec,
        out_shape=out_shape,
        scratch_shapes=[pltpu.VMEM((k, tn), BF16)],
        compiler_params=pltpu.CompilerParams(
            dimension_semantics=("arbitrary", "arbitrary"),
            vmem_limit_bytes=vmem),
        name="in_proj",
    )(x, w_t.reshape(-1, k))


def _xcast_dt_kernel(x_ref, w_ref, xb_ref, dt_ref, wb_s):
    @pl.when(pl.program_id(0) == 0)
    def _():
        wb_s[...] = w_ref[...].T.astype(BF16)

    xb = x_ref[...].astype(BF16)
    xb_ref[...] = xb
    dt_ref[...] = jnp.dot(xb, wb_s[...], preferred_element_type=F32)


def _xcast_dt(x2d, w_t, *, layer, row0, tm):
    m, k = x2d.shape
    assert row0 % SUBLANES == 0 and m % tm == 0
    row0 = layer * w_t.shape[1] + row0
    vmem = 2 * (tm * k * 4 + tm * k * 2 + LANES * k * 4 + tm * LANES * 4) + 8 * MIB
    return pl.pallas_call(
        _xcast_dt_kernel,
        grid=(m // tm,),
        in_specs=[pl.BlockSpec((tm, k), lambda i: (i, 0)),
                  pl.BlockSpec((pl.Element(LANES), pl.Element(k)), lambda i: (row0, 0))],
        out_specs=[pl.BlockSpec((tm, k), lambda i: (i, 0)),
                   pl.BlockSpec((tm, LANES), lambda i: (i, 0))],
        out_shape=[jax.ShapeDtypeStruct((m, k), BF16),
                   jax.ShapeDtypeStruct((m, LANES), F32)],
        scratch_shapes=[pltpu.VMEM((k, LANES), BF16)],
        compiler_params=pltpu.CompilerParams(
            dimension_semantics=("arbitrary",), vmem_limit_bytes=vmem),
        name="xcast_dt",
    )(x2d, w_t.reshape(-1, k))


def _ssd_kernel(z_ref, xs_ref, b_ref, c_ref, dt_ref,
                cwx_ref, cwb_ref, cwc_ref, cbx_ref, cbb_ref, cbc_ref,
                dtb_ref, alog_ref, d_ref, nw_ref, wout_ref,
                o_ref, woutb_ref,
                dt_s, acs_s, acst_s, state_s, tail_s, ext_s):
    c = pl.program_id(1)
    gstep = pl.program_id(2)
    L = SSM_CHUNK
    gs = SSD_GROUPS_PER_STEP

    woutb_ref[...] = wout_ref[...].astype(BF16)
    gw = xs_ref.shape[1] // gs

    row_id = lax.broadcasted_iota(jnp.int32, (L, LANES), 0)

    @pl.when(gstep == 0)
    def _():
        dt = _softplus(dt_ref[...] + dtb_ref[...])
        a_dt = dt * (-jnp.exp(alog_ref[...]))
        acs = a_dt
        k = 1
        while k < L:
            acs = acs + jnp.where(row_id >= k, pltpu.roll(acs, k, axis=0), 0.0)
            k *= 2
        acs2 = acs * LOG2E
        dt_s[...] = dt
        acs_s[...] = acs2
        acst_s[...] = acs2.T

    for gg in range(gs):
        _ssd_group(gg, gstep * gs + gg, c, gw,
                   z_ref, xs_ref, b_ref, c_ref, cwx_ref, cwb_ref, cwc_ref,
                   cbx_ref, cbb_ref, cbc_ref, d_ref, nw_ref, o_ref,
                   dt_s, acs_s, acst_s, state_s, tail_s, ext_s)


def _ssd_group(gg, g, c, gw, z_ref, xs_ref, b_ref, c_ref, cwx_ref, cwb_ref, cwc_ref,
               cbx_ref, cbb_ref, cbc_ref, d_ref, nw_ref, o_ref,
               dt_s, acs_s, acst_s, state_s, tail_s, ext_s):
    L = SSM_CHUNK
    heads_per_group = gw // SSM_HEADDIM
    pairs = gw // LANES
    n_xs = gw // LANES
    x0 = gg * gw
    bc0 = gg * SSM_STATE
    row_id = lax.broadcasted_iota(jnp.int32, (L, LANES), 0)
    lane_id = lax.broadcasted_iota(jnp.int32, (L, LANES), 1)

    @pl.when(c == 0)
    def _():
        state_s[g] = jnp.zeros(state_s.shape[1:], F32)
        tail_s[g] = jnp.zeros(tail_s.shape[1:], F32)

    acts = []
    for s in range(n_xs + 2):
        if s < n_xs:
            lanes = slice(x0 + s * LANES, x0 + (s + 1) * LANES)
            in_ref, w_ref, bias_ref = xs_ref, cwx_ref, cbx_ref
        else:
            lanes = slice(bc0, bc0 + LANES)
            in_ref, w_ref, bias_ref = ((b_ref, cwb_ref, cbb_ref) if s == n_xs
                                       else (c_ref, cwc_ref, cbc_ref))
        raw = in_ref[:, lanes]
        slab = gg * (n_xs + 2) + s
        ext_s[slab, 0:SUBLANES, :] = tail_s[g, :, s * LANES:(s + 1) * LANES]
        ext_s[slab, SUBLANES:SUBLANES + L, :] = raw
        tail_s[g, :, s * LANES:(s + 1) * LANES] = raw[L - SUBLANES:, :]
        acc = raw * w_ref[CONV_WIDTH - 1:CONV_WIDTH, lanes] + bias_ref[:, lanes]
        for k in range(1, CONV_WIDTH):
            acc = acc + (ext_s[slab, pl.ds(SUBLANES - k, L), :]
                         * w_ref[CONV_WIDTH - 1 - k:CONV_WIDTH - k, lanes])
        half = 0.5 * acc
        acts.append(half + half * jnp.tanh(half))
    xs = jnp.concatenate(acts[:n_xs], axis=1)
    bm = acts[n_xs].astype(BF16)
    cm = acts[n_xs + 1].astype(BF16)

    cb_mat = lax.dot_general(cm, bm, (((1,), (1,)), ((), ())),
                             preferred_element_type=F32)
    s_prev = state_s[g]
    y_off = jnp.dot(cm, s_prev.astype(BF16), preferred_element_type=F32)

    dt_all = dt_s[...]
    acs_all = acs_s[...]
    causal = row_id >= lane_id
    low = lane_id < SSM_HEADDIM

    def head_col(arr, idx):
        return jnp.sum(jnp.where(lane_id == idx, arr, 0.0), axis=1, keepdims=True)

    def head_pair(arr, idx0):
        return jnp.where(low, head_col(arr, idx0), head_col(arr, idx0 + 1))

    y_parts, xdec_parts, cdec_parts = [], [], []
    for j in range(pairs):
        i0 = g * heads_per_group + 2 * j
        i1 = i0 + 1
        col0 = head_col(acs_all, i0)
        col1 = head_col(acs_all, i1)
        row0 = acst_s[pl.ds(i0, 1), :]
        row1 = acst_s[pl.ds(i1, 1), :]
        l0 = jnp.where(causal, jnp.exp2(col0 - row0), 0.0)
        l1 = jnp.where(causal, jnp.exp2(col1 - row1), 0.0)
        m_pair = jnp.concatenate([cb_mat * l0, cb_mat * l1], axis=1).astype(BF16)

        dt_pair = head_pair(dt_all, i0)
        xs_pair = xs[:, j * LANES:(j + 1) * LANES]
        xt = xs_pair * dt_pair
        xt_b = xt.astype(BF16)
        zero = jnp.zeros_like(xt_b)
        rhs = jnp.concatenate([jnp.where(low, xt_b, zero),
                               jnp.where(low, zero, xt_b)], axis=0)
        y_diag = jnp.dot(m_pair, rhs, preferred_element_type=F32)

        scale = jnp.where(low, jnp.exp2(col0), jnp.exp2(col1))
        last0 = col0[L - 1:L, :]
        last1 = col1[L - 1:L, :]
        dec = jnp.where(low, jnp.exp2(last0 - col0), jnp.exp2(last1 - col1))
        y_parts.append(y_diag + y_off[:, j * LANES:(j + 1) * LANES] * scale
                       + d_ref[:, x0 + j * LANES:x0 + (j + 1) * LANES] * xs_pair)
        xdec_parts.append((xt * dec).astype(BF16))
        cdec_parts.append(scale[L - 1:L, :])

    xdec = jnp.concatenate(xdec_parts, axis=1)
    cdec = jnp.concatenate(cdec_parts, axis=1)
    new_states = lax.dot_general(bm, xdec, (((0,), (0,)), ((), ())),
                                 preferred_element_type=F32)
    state_s[g] = s_prev * cdec + new_states

    y = jnp.concatenate(y_parts, axis=1)
    zh = 0.5 * z_ref[:, x0:x0 + gw]
    y = y * (zh + zh * jnp.tanh(zh))
    ms = jnp.mean(y * y, axis=1, keepdims=True)
    o_ref[:, x0:x0 + gw] = (y * lax.rsqrt(ms + NORM_EPS)
                            * nw_ref[:, x0:x0 + gw]).astype(o_ref.dtype)


def _ssd_mixer(proj_a, dt_raw, conv_w, conv_b, dt_bias, a_log, d_exp, norm_w, w_out,
               *, batch, seq, d_ssm):
    L = SSM_CHUNK
    gs = SSD_GROUPS_PER_STEP
    gw = d_ssm // SSM_GROUPS
    xw = gs * gw
    sw = gs * SSM_STATE
    nchunk = seq // L
    t = batch * seq
    assert SSM_GROUPS % gs == 0
    n_gstep = SSM_GROUPS // gs
    n_step = batch * nchunk * n_gstep
    w_rows, w_cols = w_out.shape[0] // n_step, w_out.shape[1]
    assert w_out.shape[0] % n_step == 0 and w_rows % (2 * SUBLANES) == 0

    def step(b, c, g):
        return (b * nchunk + c) * n_gstep + g

    xs0 = d_ssm // xw
    b0 = 2 * d_ssm // sw
    c0 = b0 + SSM_GROUPS // gs
    cwb0 = d_ssm // sw
    cwc0 = cwb0 + SSM_GROUPS // gs

    def rows(b, c, g):
        return b * nchunk + c

    in_specs = [
        pl.BlockSpec((L, xw), lambda b, c, g: (rows(b, c, g), g)),
        pl.BlockSpec((L, xw), lambda b, c, g: (rows(b, c, g), xs0 + g)),
        pl.BlockSpec((L, sw), lambda b, c, g: (rows(b, c, g), b0 + g)),
        pl.BlockSpec((L, sw), lambda b, c, g: (rows(b, c, g), c0 + g)),
        pl.BlockSpec((L, LANES), lambda b, c, g: (rows(b, c, g), 0)),
        pl.BlockSpec((CONV_WIDTH, xw), lambda b, c, g: (0, g)),
        pl.BlockSpec((CONV_WIDTH, sw), lambda b, c, g: (0, cwb0 + g)),
        pl.BlockSpec((CONV_WIDTH, sw), lambda b, c, g: (0, cwc0 + g)),
        pl.BlockSpec((1, xw), lambda b, c, g: (0, g)),
        pl.BlockSpec((1, sw), lambda b, c, g: (0, cwb0 + g)),
        pl.BlockSpec((1, sw), lambda b, c, g: (0, cwc0 + g)),
        pl.BlockSpec((1, LANES), lambda b, c, g: (0, 0)),
        pl.BlockSpec((1, LANES), lambda b, c, g: (0, 0)),
        pl.BlockSpec((1, xw), lambda b, c, g: (0, g)),
        pl.BlockSpec((1, xw), lambda b, c, g: (0, g)),
        pl.BlockSpec((w_rows, w_cols), lambda b, c, g: (step(b, c, g), 0)),
    ]
    return pl.pallas_call(
        _ssd_kernel,
        grid=(batch, nchunk, n_gstep),
        in_specs=in_specs,
        out_specs=[pl.BlockSpec((L, xw), lambda b, c, g: (rows(b, c, g), g)),
                   pl.BlockSpec((w_rows, w_cols), lambda b, c, g: (step(b, c, g), 0))],
        out_shape=[jax.ShapeDtypeStruct((t, d_ssm), BF16),
                   jax.ShapeDtypeStruct(w_out.shape, BF16)],
        scratch_shapes=[
            pltpu.VMEM((L, LANES), F32),
            pltpu.VMEM((L, LANES), F32),
            pltpu.VMEM((LANES, L), F32),
            pltpu.VMEM((SSM_GROUPS, SSM_STATE, gw), F32),
            pltpu.VMEM((SSM_GROUPS, SUBLANES, gw + 2 * SSM_STATE), F32),
            pltpu.VMEM((gs * (gw // LANES + 2), SUBLANES + L, LANES), F32),
        ],
        compiler_params=pltpu.CompilerParams(
            dimension_semantics=("arbitrary", "arbitrary", "arbitrary"),
            vmem_limit_bytes=32 * MIB + 2 * w_rows * w_cols * 6),
        name="ssd_mixer",
    )(proj_a, proj_a, proj_a, proj_a, dt_raw,
      conv_w, conv_w, conv_w, conv_b, conv_b, conv_b,
      dt_bias, a_log, d_exp, norm_w, w_out)


def _lru_kernel(lx_ref, lg_ref, cw_ref, cb_ref, wa_ref, ba_ref, wx_ref, bx_ref,
                lam_ref, o_ref, tail_s, h_s):
    tb = pl.program_id(2)
    n_slab, ts, _ = lx_ref.shape
    n_group = ts // LRU_GROUP
    slabs_per_head = LRU_BLOCK // LANES
    S = SUBLANES
    vshape = (S, LANES)

    @pl.when(tb == 0)
    def _():
        tail_s[...] = jnp.zeros(tail_s.shape, F32)
        h_s[...] = jnp.zeros(h_s.shape, F32)

    sub = lax.broadcasted_iota(jnp.int32, vshape, 0)
    seg0 = sub == 0

    def seg_rows(ref, jj, row0):
        return ref[jj, pl.ds(row0, S, stride=S), :]

    u = []
    for jj in range(n_slab):
        lanes = slice(jj * LANES, (jj + 1) * LANES)
        taps = [jnp.broadcast_to(cw_ref[k:k + 1, lanes], vshape) for k in range(CONV_WIDTH)]
        bias = jnp.broadcast_to(cb_ref[:, lanes], vshape)
        u_slab = []
        for g in range(n_group):
            g0 = g * LRU_GROUP
            x_t = [seg_rows(lx_ref, jj, g0 + t) for t in range(S)]
            before = []
            for d in range(CONV_WIDTH - 1, 0, -1):
                if g == 0:
                    prev_row = jnp.broadcast_to(tail_s[jj, S - d:S - d + 1, :], vshape)
                    before.append(jnp.where(seg0, prev_row,
                                            pltpu.roll(x_t[S - d], 1, axis=0)))
                else:
                    before.append(seg_rows(lx_ref, jj, g0 - d))
            ext = before + x_t
            u_g = []
            for t in range(S):
                acc = ext[t + 3] * taps[3] + bias
                for k in range(CONV_WIDTH - 1):
                    acc = acc + ext[t + k] * taps[k]
                u_g.append(acc)
            u_slab.append(u_g)
        tail_s[jj] = lx_ref[jj, ts - S:ts, :]
        u.append(u_slab)

    pre_r, pre_i = [], []
    for h in range(n_slab // slabs_per_head):
        lhs = jnp.concatenate(
            [jnp.concatenate([u[h * slabs_per_head + q][g][t] for q in range(slabs_per_head)],
                             axis=1)
             for g in range(n_group) for t in range(S)], axis=0).astype(BF16)
        pre_r.append(jnp.dot(lhs, wa_ref[h], preferred_element_type=F32))
        pre_i.append(jnp.dot(lhs, wx_ref[h], preferred_element_type=F32))

    for jj in range(n_slab):
        lanes = slice(jj * LANES, (jj + 1) * LANES)
        h_idx, q = divmod(jj, slabs_per_head)
        qlanes = slice(q * LANES, (q + 1) * LANES)
        half_coef = jnp.broadcast_to(-0.5 * LRU_C * _softplus(-lam_ref[:, lanes]), vshape)
        b_r = jnp.broadcast_to(ba_ref[:, lanes], vshape)
        b_i = jnp.broadcast_to(bx_ref[:, lanes], vshape)
        carry = h_s[jj]
        for g in range(n_group):
            g0 = g * LRU_GROUP
            a_t, h_t = [], []
            for t in range(S):
                r0 = g0 + t * S
                t_r = jnp.tanh(pre_r[h_idx][r0:r0 + S, qlanes] + b_r)
                t_i = jnp.tanh(pre_i[h_idx][r0:r0 + S, qlanes] + b_i)
                log_a = half_coef * t_r + half_coef
                a = jnp.exp(log_a)
                th = jnp.tanh(log_a)
                q4 = (-0.5 * th) / (1.0 - th)
                root = jnp.where(q4 > 0.0, q4 * lax.rsqrt(q4), 0.0)
                bv = root * ((t_i + 1.0) * u[jj][g][t])
                if t == 0:
                    a_t.append(a)
                    h_t.append(bv)
                else:
                    a_t.append(a * a_t[-1])
                    h_t.append(a * h_t[-1] + bv)
            a_e, h_e = a_t[-1], h_t[-1]
            k = 1
            while k < S:
                keep = sub >= k
                a_sh = jnp.where(keep, pltpu.roll(a_e, k, axis=0), 1.0)
                h_sh = jnp.where(keep, pltpu.roll(h_e, k, axis=0), 0.0)
                h_e = a_e * h_sh + h_e
                a_e = a_e * a_sh
                k *= 2
            ends = h_e + a_e * carry
            seg_in = jnp.where(seg0, carry, pltpu.roll(ends, 1, axis=0))
            carry = jnp.broadcast_to(ends[S - 1:S, :], vshape)
            for t in range(S):
                hv = h_t[t] + a_t[t] * seg_in
                lh = 0.5 * seg_rows(lg_ref, jj, g0 + t)
                silu = lh + lh * jnp.tanh(lh)
                o_ref[jj, pl.ds(g0 + t, S, stride=S), :] = hv * silu
        h_s[jj] = carry


def _lru_mixer(proj_l, conv_w, conv_b, wa, ba, wx, bx, lam, *, batch, seq, d_lru, ts, cbw):
    t = batch * seq
    nt = seq // ts
    ncb = d_lru // cbw
    n_slab = cbw // LANES
    hb = cbw // LRU_BLOCK
    assert ts % LRU_GROUP == 0 and seq % ts == 0
    vec = pl.BlockSpec((1, cbw), lambda b, j, s: (0, j))
    return pl.pallas_call(
        _lru_kernel,
        grid=(batch, ncb, nt),
        in_specs=[
            pl.BlockSpec((n_slab, ts, LANES), lambda b, j, s: (j, b * nt + s, 0)),
            pl.BlockSpec((n_slab, ts, LANES), lambda b, j, s: (ncb + j, b * nt + s, 0)),
            pl.BlockSpec((CONV_WIDTH, cbw), lambda b, j, s: (0, j)),
            vec,
            pl.BlockSpec((hb, LRU_BLOCK, LRU_BLOCK), lambda b, j, s: (j, 0, 0)),
            vec,
            pl.BlockSpec((hb, LRU_BLOCK, LRU_BLOCK), lambda b, j, s: (j, 0, 0)),
            vec,
            vec,
        ],
        out_specs=pl.BlockSpec((n_slab, ts, LANES), lambda b, j, s: (j, b * nt + s, 0)),
        out_shape=jax.ShapeDtypeStruct((d_lru // LANES, t, LANES), F32),
        scratch_shapes=[pltpu.VMEM((n_slab, SUBLANES, LANES), F32),
                        pltpu.VMEM((n_slab, SUBLANES, LANES), F32)],
        compiler_params=pltpu.CompilerParams(
            dimension_semantics=("arbitrary", "arbitrary", "arbitrary"),
            vmem_limit_bytes=32 * MIB),
        name="lru_mixer",
    )(proj_l, proj_l, conv_w, conv_b, wa, ba, wx, bx, lam)


def _out_kernel(ssd_ref, lru_ref, w_ref, x_hbm, g_ref, b_ref, o_hbm, acc_s, xch_s,
                xsem, osem, *, nk_half, alpha):
    i = pl.program_id(0)
    k = pl.program_id(1)
    tm, d_model = acc_s.shape
    n_chunk = tm // LN_ROWS
    n_slot = xch_s.shape[0]

    def x_copy(c, slot):
        rows = pl.ds(pl.multiple_of(i * tm + c * LN_ROWS, LN_ROWS), LN_ROWS)
        return pltpu.make_async_copy(x_hbm.at[rows, :], xch_s.at[slot], xsem.at[slot])

    def o_copy(c):
        src = pl.ds(pl.multiple_of(c * LN_ROWS, LN_ROWS), LN_ROWS)
        dst = pl.ds(pl.multiple_of(i * tm + c * LN_ROWS, LN_ROWS), LN_ROWS)
        return pltpu.make_async_copy(acc_s.at[src, :], o_hbm.at[dst, :], osem)

    def accumulate(lhs, first):
        for n in range(0, d_model, OUT_N_CHUNK):
            sl = slice(n, n + OUT_N_CHUNK)
            part = jnp.dot(lhs, w_ref[:, sl], preferred_element_type=F32)
            if first:
                acc_s[:, sl] = part
            else:
                acc_s[:, sl] += part

    @pl.when(k == 0)
    def _():
        for c in range(n_slot):
            x_copy(c, c).start()
        accumulate(ssd_ref[...], True)

    @pl.when(jnp.logical_and(k > 0, k < nk_half))
    def _():
        accumulate(ssd_ref[...], False)

    @pl.when(k >= nk_half)
    def _():
        lhs = jnp.concatenate([lru_ref[jj] for jj in range(lru_ref.shape[0])], axis=1)
        accumulate(lhs.astype(BF16), False)

    @pl.when(k == 2 * nk_half - 1)
    def _():
        def ln_chunk(c, carry):
            slot = c % n_slot
            x_copy(c, slot).wait()
            rows = pl.ds(pl.multiple_of(c * LN_ROWS, LN_ROWS), LN_ROWS)
            res = acc_s[rows, :] + alpha * xch_s[slot]
            mu = jnp.mean(res, axis=1, keepdims=True)
            cen = res - mu
            var = jnp.mean(cen * cen, axis=1, keepdims=True)
            acc_s[rows, :] = cen * lax.rsqrt(var + NORM_EPS) * g_ref[...] + b_ref[...]
            o_copy(c).start()

            @pl.when(c + n_slot < n_chunk)
            def _():
                x_copy(c + n_slot, slot).start()

            return carry

        lax.fori_loop(0, n_chunk, ln_chunk, 0)

        def drain(c, carry):
            o_copy(c).wait()
            return carry

        lax.fori_loop(0, n_chunk, drain, 0)


def _out_proj(ssd_out, lru_out, w_out, x2d, ln_g, ln_b, *, alpha, tm, tk):
    t, d_half = ssd_out.shape
    d_model = x2d.shape[1]
    nk_half = d_half // tk
    ks = tk // LANES
    n_slot = min(OUT_X_SLOTS, tm // LN_ROWS)
    vmem = (2 * (tm * tk * 2 + tm * tk * 4 + tk * d_model * 2) + tm * d_model * 4
            + n_slot * LN_ROWS * d_model * 4 + tm * tk * 2 + tm * OUT_N_CHUNK * 4 + 6 * MIB)
    return pl.pallas_call(
        functools.partial(_out_kernel, nk_half=nk_half, alpha=alpha),
        grid=(t // tm, 2 * nk_half),
        in_specs=[
            pl.BlockSpec((tm, tk), lambda i, k: (i, jnp.minimum(k, nk_half - 1))),
            pl.BlockSpec((ks, tm, LANES), lambda i, k: (jnp.maximum(k - nk_half, 0), i, 0)),
            pl.BlockSpec((tk, d_model), lambda i, k: (k, 0)),
            pl.BlockSpec(memory_space=pl.ANY),
            pl.BlockSpec((1, d_model), lambda i, k: (0, 0)),
            pl.BlockSpec((1, d_model), lambda i, k: (0, 0)),
        ],
        out_specs=pl.BlockSpec(memory_space=pl.ANY),
        out_shape=jax.ShapeDtypeStruct((t, d_model), F32),
        scratch_shapes=[pltpu.VMEM((tm, d_model), F32),
                        pltpu.VMEM((n_slot, LN_ROWS, d_model), F32),
                        pltpu.SemaphoreType.DMA((n_slot,)),
                        pltpu.SemaphoreType.DMA(())],
        compiler_params=pltpu.CompilerParams(
            dimension_semantics=("arbitrary", "arbitrary"),
            vmem_limit_bytes=vmem),
        name="out_proj_ln",
    )(ssd_out, lru_out, w_out, x2d, ln_g, ln_b)


def _layer(x2d, layer, w_in, ssd_conv_w, ssd_conv_b, ssd_dt_bias, ssd_a_log, ssd_d,
           ssd_norm_w, lru_conv_w, lru_conv_b, lru_wa, lru_ba, lru_wx, lru_bx, lru_lambda,
           w_out, ln_g, ln_b, *, batch, seq, alpha):
    t, d_model = x2d.shape
    d_ssm = d_model
    d_lru = d_model
    heads = d_ssm // SSM_HEADDIM
    d_xbc = d_ssm + 2 * SSM_GROUPS * SSM_STATE
    n_a = d_ssm + d_xbc
    assert heads <= LANES and seq % SSM_CHUNK == 0

    w_t = jnp.swapaxes(w_in, 1, 2)

    tm = min(1024, t)
    xb, dt_raw = _xcast_dt(x2d, w_t, layer=layer, row0=n_a, tm=min(512, t))
    proj_a = _matmul(xb, w_t, layer=layer, row0=0, n_cols=n_a, tm=tm, tn=512)
    proj_l = _matmul(xb, w_t, layer=layer, row0=n_a + heads, n_cols=2 * d_lru, tm=tm,
                     tn=512, slab_out=True)

    pad_h = (0, LANES - heads)
    ssd_out, w_out_b = _ssd_mixer(
        proj_a, dt_raw, ssd_conv_w, ssd_conv_b.reshape(1, d_xbc),
        jnp.pad(ssd_dt_bias, pad_h).reshape(1, LANES),
        jnp.pad(ssd_a_log, pad_h).reshape(1, LANES),
        jnp.repeat(ssd_d, SSM_HEADDIM).reshape(1, d_ssm),
        ssd_norm_w.reshape(1, d_ssm), w_out,
        batch=batch, seq=seq, d_ssm=d_ssm)

    lru_out = _lru_mixer(
        proj_l, lru_conv_w, lru_conv_b.reshape(1, d_lru),
        (0.5 * lru_wa).astype(BF16), 0.5 * lru_ba.reshape(1, d_lru),
        (0.5 * lru_wx).astype(BF16), 0.5 * lru_bx.reshape(1, d_lru),
        lru_lambda.reshape(1, d_lru),
        batch=batch, seq=seq, d_lru=d_lru, ts=min(1024, seq), cbw=512)

    return _out_proj(ssd_out, lru_out, w_out_b, x2d,
                     ln_g.reshape(1, d_model), ln_b.reshape(1, d_model),
                     alpha=alpha, tm=min(1024, t), tk=1024)


def kernel(x, w_in, ssd_conv_w, ssd_conv_b, ssd_dt_bias, ssd_a_log, ssd_d, ssd_norm_w,
           lru_conv_w, lru_conv_b, lru_wa, lru_ba, lru_wx, lru_bx, lru_lambda,
           w_out, ln_g, ln_b):
    batch, seq, d_model = x.shape
    depth = w_in.shape[0]
    alpha = (2.0 * depth) ** 0.25
    h = x.reshape(batch * seq, d_model)
    for layer in range(depth):
        h = _layer(h, layer, w_in, ssd_conv_w[layer], ssd_conv_b[layer],
                   ssd_dt_bias[layer], ssd_a_log[layer], ssd_d[layer], ssd_norm_w[layer],
                   lru_conv_w[layer], lru_conv_b[layer], lru_wa[layer], lru_ba[layer],
                   lru_wx[layer], lru_bx[layer], lru_lambda[layer],
                   w_out[layer], ln_g[layer], ln_b[layer],
                   batch=batch, seq=seq, alpha=alpha)
    return h.reshape(batch, seq, d_model)
```

```python
import functools

import jax
import jax.numpy as jnp
from jax import lax
from jax.experimental import pallas as pl
from jax.experimental.pallas import tpu as pltpu

F32 = jnp.float32
BF16 = jnp.bfloat16

SSM_HEADDIM = 64
SSM_GROUPS = 8
SSM_STATE = 128
SSM_CHUNK = 128
LRU_BLOCK = 256
LRU_C = 8.0
CONV_WIDTH = 4
NORM_EPS = 1e-5
LOG2E = 1.4426950408889634

LANES = 128
SUBLANES = 8
MIB = 1024 * 1024
OUT_N_CHUNK = 512
LN_ROWS = 64
OUT_X_SLOTS = 4
LRU_GROUP = SUBLANES * SUBLANES
SSD_GROUPS_PER_STEP = 8

def _softplus(v):
    return jnp.maximum(v, 0.0) + jnp.log1p(jnp.exp(-jnp.abs(v)))


def _mm_kernel(x_ref, w_ref, side_ref, o_ref, side_out_ref, wb_s):
    @pl.when(pl.program_id(1) == 0)
    def _():
        for r in range(0, w_ref.shape[0], LANES):
            wb_s[:, r:r + LANES] = w_ref[r:r + LANES, :].T.astype(BF16)

    side_out_ref[...] = side_ref[...].astype(BF16)
    o_ref[...] = jnp.dot(x_ref[...], wb_s[...], preferred_element_type=F32)


def _matmul_and_cast(x, w_t, *, layer, row0, n_cols, side_row0, side_rows, tm, tn):
    m, k = x.shape
    n_j, n_i = n_cols // tn, m // tm
    assert row0 % SUBLANES == 0 and n_cols % tn == 0 and m % tm == 0 and tn % LANES == 0
    row0 = layer * w_t.shape[1] + row0
    side_row0 = layer * w_t.shape[1] + side_row0
    bf16_rows = 2 * SUBLANES
    sr = next(r for r in range(bf16_rows, side_rows + 1, bf16_rows)
              if side_rows % r == 0 and side_rows // r <= n_j * n_i)
    n_side = side_rows // sr

    def side_block(j, i):
        return jnp.minimum(j * n_i + i, n_side - 1)

    w_spec = pl.BlockSpec((pl.Element(tn), pl.Element(k)),
                          lambda j, i: (pl.multiple_of(row0 + j * tn, SUBLANES), 0))
    side_spec = pl.BlockSpec(
        (pl.Element(sr), pl.Element(k)),
        lambda j, i: (pl.multiple_of(side_row0 + side_block(j, i) * sr, SUBLANES), 0))
    vmem = (2 * (tm * k * 2 + tn * k * 4 + tm * tn * 4 + sr * k * 6) + tn * k * 2
            + tm * tn * 4 + 6 * MIB)
    w_flat = w_t.reshape(-1, k)
    return pl.pallas_call(
        _mm_kernel,
        grid=(n_j, n_i),
        in_specs=[pl.BlockSpec((tm, k), lambda j, i: (i, 0)), w_spec, side_spec],
        out_specs=[pl.BlockSpec((tm, tn), lambda j, i: (i, j)),
                   pl.BlockSpec((sr, k), lambda j, i: (side_block(j, i), 0))],
        out_shape=[jax.ShapeDtypeStruct((m, n_cols), F32),
                   jax.ShapeDtypeStruct((side_rows, k), BF16)],
        scratch_shapes=[pltpu.VMEM((k, tn), BF16)],
        compiler_params=pltpu.CompilerParams(
            dimension_semantics=("arbitrary", "arbitrary"),
            vmem_limit_bytes=vmem),
        name="in_proj",
    )(x, w_flat, w_flat)


def _mm_bf16_kernel(x_ref, w_ref, o_ref):
    res = lax.dot_general(x_ref[...], w_ref[...], (((1,), (1,)), ((), ())),
                          preferred_element_type=F32)
    for jj in range(o_ref.shape[0]):
        o_ref[jj] = res[:, jj * LANES:(jj + 1) * LANES]


def _matmul_slab(x, w_b, *, tm, tn):
    m, k = x.shape
    n = w_b.shape[0]
    assert n % tn == 0 and m % tm == 0 and tn % LANES == 0
    vmem = 2 * (tm * k * 2 + tn * k * 2 + tm * tn * 4) + tm * tn * 4 + 6 * MIB
    return pl.pallas_call(
        _mm_bf16_kernel,
        grid=(n // tn, m // tm),
        in_specs=[pl.BlockSpec((tm, k), lambda j, i: (i, 0)),
                  pl.BlockSpec((tn, k), lambda j, i: (j, 0))],
        out_specs=pl.BlockSpec((tn // LANES, tm, LANES), lambda j, i: (j, i, 0)),
        out_shape=jax.ShapeDtypeStruct((n // LANES, m, LANES), F32),
        compiler_params=pltpu.CompilerParams(
            dimension_semantics=("arbitrary", "arbitrary"),
            vmem_limit_bytes=vmem),
        name="in_proj_lru",
    )(x, w_b)


def _xcast_dt_kernel(x_ref, w_ref, xb_ref, dt_ref, wb_s):
    @pl.when(pl.program_id(0) == 0)
    def _():
        wb_s[...] = w_ref[...].T.astype(BF16)

    xb = x_ref[...].astype(BF16)
    xb_ref[...] = xb
    dt_ref[...] = jnp.dot(xb, wb_s[...], preferred_element_type=F32)


def _xcast_dt(x2d, w_t, *, layer, row0, tm):
    m, k = x2d.shape
    assert row0 % SUBLANES == 0 and m % tm == 0
    row0 = layer * w_t.shape[1] + row0
    vmem = 2 * (tm * k * 4 + tm * k * 2 + LANES * k * 4 + tm * LANES * 4) + 8 * MIB
    return pl.pallas_call(
        _xcast_dt_kernel,
        grid=(m // tm,),
        in_specs=[pl.BlockSpec((tm, k), lambda i: (i, 0)),
                  pl.BlockSpec((pl.Element(LANES), pl.Element(k)), lambda i: (row0, 0))],
        out_specs=[pl.BlockSpec((tm, k), lambda i: (i, 0)),
                   pl.BlockSpec((tm, LANES), lambda i: (i, 0))],
        out_shape=[jax.ShapeDtypeStruct((m, k), BF16),
                   jax.ShapeDtypeStruct((m, LANES), F32)],
        scratch_shapes=[pltpu.VMEM((k, LANES), BF16)],
        compiler_params=pltpu.CompilerParams(
            dimension_semantics=("arbitrary",), vmem_limit_bytes=vmem),
        name="xcast_dt",
    )(x2d, w_t.reshape(-1, k))


def _ssd_kernel(z_ref, xs_ref, b_ref, c_ref, dt_ref,
                cwx_ref, cwb_ref, cwc_ref, cbx_ref, cbb_ref, cbc_ref,
                dtb_ref, alog_ref, d_ref, nw_ref, wout_ref,
                o_ref, woutb_ref,
                dt_s, acs_s, acst_s, state_s, tail_s, ext_s):
    c = pl.program_id(1)
    gstep = pl.program_id(2)
    L = SSM_CHUNK
    gs = SSD_GROUPS_PER_STEP

    woutb_ref[...] = wout_ref[...].astype(BF16)
    gw = xs_ref.shape[1] // gs

    row_id = lax.broadcasted_iota(jnp.int32, (L, LANES), 0)

    @pl.when(gstep == 0)
    def _():
        dt = _softplus(dt_ref[...] + dtb_ref[...])
        a_dt = dt * (-jnp.exp(alog_ref[...]))
        acs = a_dt
        k = 1
        while k < L:
            acs = acs + jnp.where(row_id >= k, pltpu.roll(acs, k, axis=0), 0.0)
            k *= 2
        acs2 = acs * LOG2E
        dt_s[...] = dt
        acs_s[...] = acs2
        acst_s[...] = acs2.T

    for gg in range(gs):
        _ssd_group(gg, gstep * gs + gg, c, gw,
                   z_ref, xs_ref, b_ref, c_ref, cwx_ref, cwb_ref, cwc_ref,
                   cbx_ref, cbb_ref, cbc_ref, d_ref, nw_ref, o_ref,
                   dt_s, acs_s, acst_s, state_s, tail_s, ext_s)


def _ssd_group(gg, g, c, gw, z_ref, xs_ref, b_ref, c_ref, cwx_ref, cwb_ref, cwc_ref,
               cbx_ref, cbb_ref, cbc_ref, d_ref, nw_ref, o_ref,
               dt_s, acs_s, acst_s, state_s, tail_s, ext_s):
    L = SSM_CHUNK
    heads_per_group = gw // SSM_HEADDIM
    pairs = gw // LANES
    n_xs = gw // LANES
    x0 = gg * gw
    bc0 = gg * SSM_STATE
    row_id = lax.broadcasted_iota(jnp.int32, (L, LANES), 0)
    lane_id = lax.broadcasted_iota(jnp.int32, (L, LANES), 1)

    @pl.when(c == 0)
    def _():
        state_s[g] = jnp.zeros(state_s.shape[1:], F32)
        tail_s[g] = jnp.zeros(tail_s.shape[1:], F32)

    acts = []
    for s in range(n_xs + 2):
        if s < n_xs:
            lanes = slice(x0 + s * LANES, x0 + (s + 1) * LANES)
            in_ref, w_ref, bias_ref = xs_ref, cwx_ref, cbx_ref
        else:
            lanes = slice(bc0, bc0 + LANES)
            in_ref, w_ref, bias_ref = ((b_ref, cwb_ref, cbb_ref) if s == n_xs
                                       else (c_ref, cwc_ref, cbc_ref))
        raw = in_ref[:, lanes]
        slab = gg * (n_xs + 2) + s
        ext_s[slab, 0:SUBLANES, :] = tail_s[g, :, s * LANES:(s + 1) * LANES]
        ext_s[slab, SUBLANES:SUBLANES + L, :] = raw
        tail_s[g, :, s * LANES:(s + 1) * LANES] = raw[L - SUBLANES:, :]
        acc = raw * w_ref[CONV_WIDTH - 1:CONV_WIDTH, lanes] + bias_ref[:, lanes]
        for k in range(1, CONV_WIDTH):
            acc = acc + (ext_s[slab, pl.ds(SUBLANES - k, L), :]
                         * w_ref[CONV_WIDTH - 1 - k:CONV_WIDTH - k, lanes])
        half = 0.5 * acc
        acts.append(half + half * jnp.tanh(half))
    xs = jnp.concatenate(acts[:n_xs], axis=1)
    bm = acts[n_xs].astype(BF16)
    cm = acts[n_xs + 1].astype(BF16)

    cb_mat = lax.dot_general(cm, bm, (((1,), (1,)), ((), ())),
                             preferred_element_type=F32)
    s_prev = state_s[g]
    y_off = jnp.dot(cm, s_prev.astype(BF16), preferred_element_type=F32)

    dt_all = dt_s[...]
    acs_all = acs_s[...]
    causal = row_id >= lane_id
    low = lane_id < SSM_HEADDIM

    def head_col(arr, idx):
        return jnp.sum(jnp.where(lane_id == idx, arr, 0.0), axis=1, keepdims=True)

    def head_pair(arr, idx0):
        return jnp.where(low, head_col(arr, idx0), head_col(arr, idx0 + 1))

    y_parts, xdec_parts, cdec_parts = [], [], []
    for j in range(pairs):
        i0 = g * heads_per_group + 2 * j
        i1 = i0 + 1
        col0 = head_col(acs_all, i0)
        col1 = head_col(acs_all, i1)
        row0 = acst_s[pl.ds(i0, 1), :]
        row1 = acst_s[pl.ds(i1, 1), :]
        l0 = jnp.where(causal, jnp.exp2(col0 - row0), 0.0)
        l1 = jnp.where(causal, jnp.exp2(col1 - row1), 0.0)
        m_pair = jnp.concatenate([cb_mat * l0, cb_mat * l1], axis=1).astype(BF16)

        dt_pair = head_pair(dt_all, i0)
        xs_pair = xs[:, j * LANES:(j + 1) * LANES]
        xt = xs_pair * dt_pair
        xt_b = xt.astype(BF16)
        zero = jnp.zeros_like(xt_b)
        rhs = jnp.concatenate([jnp.where(low, xt_b, zero),
                               jnp.where(low, zero, xt_b)], axis=0)
        y_diag = jnp.dot(m_pair, rhs, preferred_element_type=F32)

        scale = jnp.where(low, jnp.exp2(col0), jnp.exp2(col1))
        last0 = col0[L - 1:L, :]
        last1 = col1[L - 1:L, :]
        dec = jnp.where(low, jnp.exp2(last0 - col0), jnp.exp2(last1 - col1))
        y_parts.append(y_diag + y_off[:, j * LANES:(j + 1) * LANES] * scale
                       + d_ref[:, x0 + j * LANES:x0 + (j + 1) * LANES] * xs_pair)
        xdec_parts.append((xt * dec).astype(BF16))
        cdec_parts.append(scale[L - 1:L, :])

    xdec = jnp.concatenate(xdec_parts, axis=1)
    cdec = jnp.concatenate(cdec_parts, axis=1)
    new_states = lax.dot_general(bm, xdec, (((0,), (0,)), ((), ())),
                                 preferred_element_type=F32)
    state_s[g] = s_prev * cdec + new_states

    y = jnp.concatenate(y_parts, axis=1)
    zh = 0.5 * z_ref[:, x0:x0 + gw]
    y = y * (zh + zh * jnp.tanh(zh))
    ms = jnp.mean(y * y, axis=1, keepdims=True)
    o_ref[:, x0:x0 + gw] = (y * lax.rsqrt(ms + NORM_EPS)
                            * nw_ref[:, x0:x0 + gw]).astype(o_ref.dtype)


def _ssd_mixer(proj_a, dt_raw, conv_w, conv_b, dt_bias, a_log, d_exp, norm_w, w_out,
               *, batch, seq, d_ssm):
    L = SSM_CHUNK
    gs = SSD_GROUPS_PER_STEP
    gw = d_ssm // SSM_GROUPS
    xw = gs * gw
    sw = gs * SSM_STATE
    nchunk = seq // L
    t = batch * seq
    assert SSM_GROUPS % gs == 0
    n_gstep = SSM_GROUPS // gs
    n_step = batch * nchunk * n_gstep
    w_rows, w_cols = w_out.shape[0] // n_step, w_out.shape[1]
    assert w_out.shape[0] % n_step == 0 and w_rows % (2 * SUBLANES) == 0

    def step(b, c, g):
        return (b * nchunk + c) * n_gstep + g

    xs0 = d_ssm // xw
    b0 = 2 * d_ssm // sw
    c0 = b0 + SSM_GROUPS // gs
    cwb0 = d_ssm // sw
    cwc0 = cwb0 + SSM_GROUPS // gs

    def rows(b, c, g):
        return b * nchunk + c

    in_specs = [
        pl.BlockSpec((L, xw), lambda b, c, g: (rows(b, c, g), g)),
        pl.BlockSpec((L, xw), lambda b, c, g: (rows(b, c, g), xs0 + g)),
        pl.BlockSpec((L, sw), lambda b, c, g: (rows(b, c, g), b0 + g)),
        pl.BlockSpec((L, sw), lambda b, c, g: (rows(b, c, g), c0 + g)),
        pl.BlockSpec((L, LANES), lambda b, c, g: (rows(b, c, g), 0)),
        pl.BlockSpec((CONV_WIDTH, xw), lambda b, c, g: (0, g)),
        pl.BlockSpec((CONV_WIDTH, sw), lambda b, c, g: (0, cwb0 + g)),
        pl.BlockSpec((CONV_WIDTH, sw), lambda b, c, g: (0, cwc0 + g)),
        pl.BlockSpec((1, xw), lambda b, c, g: (0, g)),
        pl.BlockSpec((1, sw), lambda b, c, g: (0, cwb0 + g)),
        pl.BlockSpec((1, sw), lambda b, c, g: (0, cwc0 + g)),
        pl.BlockSpec((1, LANES), lambda b, c, g: (0, 0)),
        pl.BlockSpec((1, LANES), lambda b, c, g: (0, 0)),
        pl.BlockSpec((1, xw), lambda b, c, g: (0, g)),
        pl.BlockSpec((1, xw), lambda b, c, g: (0, g)),
        pl.BlockSpec((w_rows, w_cols), lambda b, c, g: (step(b, c, g), 0)),
    ]
    return pl.pallas_call(
        _ssd_kernel,
        grid=(batch, nchunk, n_gstep),
        in_specs=in_specs,
        out_specs=[pl.BlockSpec((L, xw), lambda b, c, g: (rows(b, c, g), g)),
                   pl.BlockSpec((w_rows, w_cols), lambda b, c, g: (step(b, c, g), 0))],
        out_shape=[jax.ShapeDtypeStruct((t, d_ssm), BF16),
                   jax.ShapeDtypeStruct(w_out.shape, BF16)],
        scratch_shapes=[
            pltpu.VMEM((L, LANES), F32),
            pltpu.VMEM((L, LANES), F32),
            pltpu.VMEM((LANES, L), F32),
            pltpu.VMEM((SSM_GROUPS, SSM_STATE, gw), F32),
            pltpu.VMEM((SSM_GROUPS, SUBLANES, gw + 2 * SSM_STATE), F32),
            pltpu.VMEM((gs * (gw // LANES + 2), SUBLANES + L, LANES), F32),
        ],
        compiler_params=pltpu.CompilerParams(
            dimension_semantics=("arbitrary", "arbitrary", "arbitrary"),
            vmem_limit_bytes=32 * MIB + 2 * w_rows * w_cols * 6),
        name="ssd_mixer",
    )(proj_a, proj_a, proj_a, proj_a, dt_raw,
      conv_w, conv_w, conv_w, conv_b, conv_b, conv_b,
      dt_bias, a_log, d_exp, norm_w, w_out)


def _lru_kernel(lx_ref, lg_ref, cw_ref, cb_ref, wa_ref, ba_ref, wx_ref, bx_ref,
                lam_ref, o_ref, tail_s, h_s):
    tb = pl.program_id(2)
    n_slab, ts, _ = lx_ref.shape
    n_group = ts // LRU_GROUP
    slabs_per_head = LRU_BLOCK // LANES
    S = SUBLANES
    vshape = (S, LANES)

    @pl.when(tb == 0)
    def _():
        tail_s[...] = jnp.zeros(tail_s.shape, F32)
        h_s[...] = jnp.zeros(h_s.shape, F32)

    sub = lax.broadcasted_iota(jnp.int32, vshape, 0)
    seg0 = sub == 0

    def seg_rows(ref, jj, row0):
        return ref[jj, pl.ds(row0, S, stride=S), :]

    u = []
    for jj in range(n_slab):
        lanes = slice(jj * LANES, (jj + 1) * LANES)
        taps = [jnp.broadcast_to(cw_ref[k:k + 1, lanes], vshape) for k in range(CONV_WIDTH)]
        bias = jnp.broadcast_to(cb_ref[:, lanes], vshape)
        u_slab = []
        for g in range(n_group):
            g0 = g * LRU_GROUP
            x_t = [seg_rows(lx_ref, jj, g0 + t) for t in range(S)]
            before = []
            for d in range(CONV_WIDTH - 1, 0, -1):
                if g == 0:
                    prev_row = jnp.broadcast_to(tail_s[jj, S - d:S - d + 1, :], vshape)
                    before.append(jnp.where(seg0, prev_row,
                                            pltpu.roll(x_t[S - d], 1, axis=0)))
                else:
                    before.append(seg_rows(lx_ref, jj, g0 - d))
            ext = before + x_t
            u_g = []
            for t in range(S):
                acc = ext[t + 3] * taps[3] + bias
                for k in range(CONV_WIDTH - 1):
                    acc = acc + ext[t + k] * taps[k]
                u_g.append(acc)
            u_slab.append(u_g)
        tail_s[jj] = lx_ref[jj, ts - S:ts, :]
        u.append(u_slab)

    pre_r, pre_i = [], []
    for h in range(n_slab // slabs_per_head):
        lhs = jnp.concatenate(
            [jnp.concatenate([u[h * slabs_per_head + q][g][t] for q in range(slabs_per_head)],
                             axis=1)
             for g in range(n_group) for t in range(S)], axis=0).astype(BF16)
        pre_r.append(jnp.dot(lhs, wa_ref[h], preferred_element_type=F32))
        pre_i.append(jnp.dot(lhs, wx_ref[h], preferred_element_type=F32))

    for jj in range(n_slab):
        lanes = slice(jj * LANES, (jj + 1) * LANES)
        h_idx, q = divmod(jj, slabs_per_head)
        qlanes = slice(q * LANES, (q + 1) * LANES)
        half_coef = jnp.broadcast_to(-0.5 * LRU_C * _softplus(-lam_ref[:, lanes]), vshape)
        b_r = jnp.broadcast_to(ba_ref[:, lanes], vshape)
        b_i = jnp.broadcast_to(bx_ref[:, lanes], vshape)
        carry = h_s[jj]
        for g in range(n_group):
            g0 = g * LRU_GROUP
            a_t, h_t = [], []
            for t in range(S):
                r0 = g0 + t * S
                t_r = jnp.tanh(pre_r[h_idx][r0:r0 + S, qlanes] + b_r)
                t_i = jnp.tanh(pre_i[h_idx][r0:r0 + S, qlanes] + b_i)
                log_a = half_coef * t_r + half_coef
                a = jnp.exp(log_a)
                th = jnp.tanh(log_a)
                q4 = (-0.5 * th) / (1.0 - th)
                root = jnp.where(q4 > 0.0, q4 * lax.rsqrt(q4), 0.0)
                bv = root * ((t_i + 1.0) * u[jj][g][t])
                if t == 0:
                    a_t.append(a)
                    h_t.append(bv)
                else:
                    a_t.append(a * a_t[-1])
                    h_t.append(a * h_t[-1] + bv)
            a_e, h_e = a_t[-1], h_t[-1]
            k = 1
            while k < S:
                keep = sub >= k
                a_sh = jnp.where(keep, pltpu.roll(a_e, k, axis=0), 1.0)
                h_sh = jnp.where(keep, pltpu.roll(h_e, k, axis=0), 0.0)
                h_e = a_e * h_sh + h_e
                a_e = a_e * a_sh
                k *= 2
            ends = h_e + a_e * carry
            seg_in = jnp.where(seg0, carry, pltpu.roll(ends, 1, axis=0))
            carry = jnp.broadcast_to(ends[S - 1:S, :], vshape)
            for t in range(S):
                hv = h_t[t] + a_t[t] * seg_in
                lh = 0.5 * seg_rows(lg_ref, jj, g0 + t)
                silu = lh + lh * jnp.tanh(lh)
                o_ref[jj, pl.ds(g0 + t, S, stride=S), :] = hv * silu
        h_s[jj] = carry


def _lru_mixer(proj_l, conv_w, conv_b, wa, ba, wx, bx, lam, *, batch, seq, d_lru, ts, cbw):
    t = batch * seq
    nt = seq // ts
    ncb = d_lru // cbw
    n_slab = cbw // LANES
    hb = cbw // LRU_BLOCK
    assert ts % LRU_GROUP == 0 and seq % ts == 0
    vec = pl.BlockSpec((1, cbw), lambda b, j, s: (0, j))
    return pl.pallas_call(
        _lru_kernel,
        grid=(batch, ncb, nt),
        in_specs=[
            pl.BlockSpec((n_slab, ts, LANES), lambda b, j, s: (j, b * nt + s, 0)),
            pl.BlockSpec((n_slab, ts, LANES), lambda b, j, s: (ncb + j, b * nt + s, 0)),
            pl.BlockSpec((CONV_WIDTH, cbw), lambda b, j, s: (0, j)),
            vec,
            pl.BlockSpec((hb, LRU_BLOCK, LRU_BLOCK), lambda b, j, s: (j, 0, 0)),
            vec,
            pl.BlockSpec((hb, LRU_BLOCK, LRU_BLOCK), lambda b, j, s: (j, 0, 0)),
            vec,
            vec,
        ],
        out_specs=pl.BlockSpec((n_slab, ts, LANES), lambda b, j, s: (j, b * nt + s, 0)),
        out_shape=jax.ShapeDtypeStruct((d_lru // LANES, t, LANES), F32),
        scratch_shapes=[pltpu.VMEM((n_slab, SUBLANES, LANES), F32),
                        pltpu.VMEM((n_slab, SUBLANES, LANES), F32)],
        compiler_params=pltpu.CompilerParams(
            dimension_semantics=("arbitrary", "arbitrary", "arbitrary"),
            vmem_limit_bytes=32 * MIB),
        name="lru_mixer",
    )(proj_l, proj_l, conv_w, conv_b, wa, ba, wx, bx, lam)


def _out_kernel(ssd_ref, lru_ref, w_ref, x_hbm, g_ref, b_ref, o_hbm, acc_s, xch_s,
                xsem, osem, *, nk_half, alpha):
    i = pl.program_id(0)
    k = pl.program_id(1)
    tm, d_model = acc_s.shape
    n_chunk = tm // LN_ROWS
    n_slot = xch_s.shape[0]

    def x_copy(c, slot):
        rows = pl.ds(pl.multiple_of(i * tm + c * LN_ROWS, LN_ROWS), LN_ROWS)
        return pltpu.make_async_copy(x_hbm.at[rows, :], xch_s.at[slot], xsem.at[slot])

    def o_copy(c):
        src = pl.ds(pl.multiple_of(c * LN_ROWS, LN_ROWS), LN_ROWS)
        dst = pl.ds(pl.multiple_of(i * tm + c * LN_ROWS, LN_ROWS), LN_ROWS)
        return pltpu.make_async_copy(acc_s.at[src, :], o_hbm.at[dst, :], osem)

    def accumulate(lhs, first):
        for n in range(0, d_model, OUT_N_CHUNK):
            sl = slice(n, n + OUT_N_CHUNK)
            part = jnp.dot(lhs, w_ref[:, sl], preferred_element_type=F32)
            if first:
                acc_s[:, sl] = part
            else:
                acc_s[:, sl] += part

    @pl.when(k == 0)
    def _():
        for c in range(n_slot):
            x_copy(c, c).start()
        accumulate(ssd_ref[...], True)

    @pl.when(jnp.logical_and(k > 0, k < nk_half))
    def _():
        accumulate(ssd_ref[...], False)

    @pl.when(k >= nk_half)
    def _():
        lhs = jnp.concatenate([lru_ref[jj] for jj in range(lru_ref.shape[0])], axis=1)
        accumulate(lhs.astype(BF16), False)

    @pl.when(k == 2 * nk_half - 1)
    def _():
        def ln_chunk(c, carry):
            slot = c % n_slot
            x_copy(c, slot).wait()
            rows = pl.ds(pl.multiple_of(c * LN_ROWS, LN_ROWS), LN_ROWS)
            res = acc_s[rows, :] + alpha * xch_s[slot]
            mu = jnp.mean(res, axis=1, keepdims=True)
            cen = res - mu
            var = jnp.mean(cen * cen, axis=1, keepdims=True)
            acc_s[rows, :] = cen * lax.rsqrt(var + NORM_EPS) * g_ref[...] + b_ref[...]
            o_copy(c).start()

            @pl.when(c + n_slot < n_chunk)
            def _():
                x_copy(c + n_slot, slot).start()

            return carry

        lax.fori_loop(0, n_chunk, ln_chunk, 0)

        def drain(c, carry):
            o_copy(c).wait()
            return carry

        lax.fori_loop(0, n_chunk, drain, 0)


def _out_proj(ssd_out, lru_out, w_out, x2d, ln_g, ln_b, *, alpha, tm, tk):
    t, d_half = ssd_out.shape
    d_model = x2d.shape[1]
    nk_half = d_half // tk
    ks = tk // LANES
    n_slot = min(OUT_X_SLOTS, tm // LN_ROWS)
    vmem = (2 * (tm * tk * 2 + tm * tk * 4 + tk * d_model * 2) + tm * d_model * 4
            + n_slot * LN_ROWS * d_model * 4 + tm * tk * 2 + tm * OUT_N_CHUNK * 4 + 6 * MIB)
    return pl.pallas_call(
        functools.partial(_out_kernel, nk_half=nk_half, alpha=alpha),
        grid=(t // tm, 2 * nk_half),
        in_specs=[
            pl.BlockSpec((tm, tk), lambda i, k: (i, jnp.minimum(k, nk_half - 1))),
            pl.BlockSpec((ks, tm, LANES), lambda i, k: (jnp.maximum(k - nk_half, 0), i, 0)),
            pl.BlockSpec((tk, d_model), lambda i, k: (k, 0)),
            pl.BlockSpec(memory_space=pl.ANY),
            pl.BlockSpec((1, d_model), lambda i, k: (0, 0)),
            pl.BlockSpec((1, d_model), lambda i, k: (0, 0)),
        ],
        out_specs=pl.BlockSpec(memory_space=pl.ANY),
        out_shape=jax.ShapeDtypeStruct((t, d_model), F32),
        scratch_shapes=[pltpu.VMEM((tm, d_model), F32),
                        pltpu.VMEM((n_slot, LN_ROWS, d_model), F32),
                        pltpu.SemaphoreType.DMA((n_slot,)),
                        pltpu.SemaphoreType.DMA(())],
        compiler_params=pltpu.CompilerParams(
            dimension_semantics=("arbitrary", "arbitrary"),
            vmem_limit_bytes=vmem),
        name="out_proj_ln",
    )(ssd_out, lru_out, w_out, x2d, ln_g, ln_b)


def _layer(x2d, layer, w_in, ssd_conv_w, ssd_conv_b, ssd_dt_bias, ssd_a_log, ssd_d,
           ssd_norm_w, lru_conv_w, lru_conv_b, lru_wa, lru_ba, lru_wx, lru_bx, lru_lambda,
           w_out, ln_g, ln_b, *, batch, seq, alpha):
    t, d_model = x2d.shape
    d_ssm = d_model
    d_lru = d_model
    heads = d_ssm // SSM_HEADDIM
    d_xbc = d_ssm + 2 * SSM_GROUPS * SSM_STATE
    n_a = d_ssm + d_xbc
    assert heads <= LANES and seq % SSM_CHUNK == 0

    w_t = jnp.swapaxes(w_in, 1, 2)

    tm = min(1024, t)
    xb, dt_raw = _xcast_dt(x2d, w_t, layer=layer, row0=n_a, tm=min(512, t))
    proj_a, w_l = _matmul_and_cast(xb, w_t, layer=layer, row0=0, n_cols=n_a,
                                   side_row0=n_a + heads, side_rows=2 * d_lru,
                                   tm=tm, tn=512)
    proj_l = _matmul_slab(xb, w_l, tm=tm, tn=1024)

    pad_h = (0, LANES - heads)
    ssd_out, w_out_b = _ssd_mixer(
        proj_a, dt_raw, ssd_conv_w, ssd_conv_b.reshape(1, d_xbc),
        jnp.pad(ssd_dt_bias, pad_h).reshape(1, LANES),
        jnp.pad(ssd_a_log, pad_h).reshape(1, LANES),
        jnp.repeat(ssd_d, SSM_HEADDIM).reshape(1, d_ssm),
        ssd_norm_w.reshape(1, d_ssm), w_out,
        batch=batch, seq=seq, d_ssm=d_ssm)

    lru_out = _lru_mixer(
        proj_l, lru_conv_w, lru_conv_b.reshape(1, d_lru),
        (0.5 * lru_wa).astype(BF16), 0.5 * lru_ba.reshape(1, d_lru),
        (0.5 * lru_wx).astype(BF16), 0.5 * lru_bx.reshape(1, d_lru),
        lru_lambda.reshape(1, d_lru),
        batch=batch, seq=seq, d_lru=d_lru, ts=min(1024, seq), cbw=512)

    return _out_proj(ssd_out, lru_out, w_out_b, x2d,
                     ln_g.reshape(1, d_model), ln_b.reshape(1, d_model),
                     alpha=alpha, tm=min(1024, t), tk=1024)


def kernel(x, w_in, ssd_conv_w, ssd_conv_b, ssd_dt_bias, ssd_a_log, ssd_d, ssd_norm_w,
           lru_conv_w, lru_conv_b, lru_wa, lru_ba, lru_wx, lru_bx, lru_lambda,
           w_out, ln_g, ln_b):
    batch, seq, d_model = x.shape
    depth = w_in.shape[0]
    alpha = (2.0 * depth) ** 0.25
    h = x.reshape(batch * seq, d_model)
    for layer in range(depth):
        h = _layer(h, layer, w_in, ssd_conv_w[layer], ssd_conv_b[layer],
                   ssd_dt_bias[layer], ssd_a_log[layer], ssd_d[layer], ssd_norm_w[layer],
                   lru_conv_w[layer], lru_conv_b[layer], lru_wa[layer], lru_ba[layer],
                   lru_wx[layer], lru_bx[layer], lru_lambda[layer],
                   w_out[layer], ln_g[layer], ln_b[layer],
                   batch=batch, seq=seq, alpha=alpha)
    return h.reshape(batch, seq, d_model)
```

```python
import functools

import jax
import jax.numpy as jnp
from jax import lax
from jax.experimental import pallas as pl
from jax.experimental.pallas import tpu as pltpu

F32 = jnp.float32
BF16 = jnp.bfloat16

SSM_HEADDIM = 64
SSM_GROUPS = 8
SSM_STATE = 128
SSM_CHUNK = 128
LRU_BLOCK = 256
LRU_C = 8.0
CONV_WIDTH = 4
NORM_EPS = 1e-5
LOG2E = 1.4426950408889634

LANES = 128
SUBLANES = 8
MIB = 1024 * 1024
OUT_N_CHUNK = 512
LN_ROWS = 64
OUT_X_SLOTS = 4
LRU_GROUP = SUBLANES * SUBLANES
SSD_GROUPS_PER_STEP = 8

def _softplus(v):
    return jnp.maximum(v, 0.0) + jnp.log1p(jnp.exp(-jnp.abs(v)))


def _mm_staged_kernel(x_ref, w_ref, o_ref, wb_s):
    @pl.when(pl.program_id(1) == 0)
    def _():
        for r in range(0, w_ref.shape[0], LANES):
            wb_s[:, r:r + LANES] = w_ref[r:r + LANES, :].T.astype(BF16)

    res = jnp.dot(x_ref[...], wb_s[...], preferred_element_type=F32)
    for jj in range(o_ref.shape[0]):
        o_ref[jj] = res[:, jj * LANES:(jj + 1) * LANES]


def _matmul_staged_slab(x, w_t, *, layer, row0, n_cols, tm, tn):
    m, k = x.shape
    assert row0 % SUBLANES == 0 and n_cols % tn == 0 and m % tm == 0 and tn % LANES == 0
    row0 = layer * w_t.shape[1] + row0
    w_spec = pl.BlockSpec((pl.Element(tn), pl.Element(k)),
                          lambda j, i: (pl.multiple_of(row0 + j * tn, SUBLANES), 0))
    vmem = (2 * (tm * k * 2 + tn * k * 4 + tm * tn * 4) + tn * k * 2 + tm * tn * 4
            + 6 * MIB)
    return pl.pallas_call(
        _mm_staged_kernel,
        grid=(n_cols // tn, m // tm),
        in_specs=[pl.BlockSpec((tm, k), lambda j, i: (i, 0)), w_spec],
        out_specs=pl.BlockSpec((tn // LANES, tm, LANES), lambda j, i: (j, i, 0)),
        out_shape=jax.ShapeDtypeStruct((n_cols // LANES, m, LANES), F32),
        scratch_shapes=[pltpu.VMEM((k, tn), BF16)],
        compiler_params=pltpu.CompilerParams(
            dimension_semantics=("arbitrary", "arbitrary"),
            vmem_limit_bytes=vmem),
        name="in_proj_lru",
    )(x, w_t.reshape(-1, k))


def _mm_bf16_kernel(x_ref, w_ref, o_ref):
    o_ref[...] = lax.dot_general(x_ref[...], w_ref[...], (((1,), (1,)), ((), ())),
                                 preferred_element_type=F32)


def _matmul_bf16(x, w_b, *, tm, tn):
    m, k = x.shape
    n = w_b.shape[0]
    assert n % tn == 0 and m % tm == 0 and tn % LANES == 0
    vmem = 2 * (tm * k * 2 + tn * k * 2 + tm * tn * 4) + tm * tn * 4 + 6 * MIB
    return pl.pallas_call(
        _mm_bf16_kernel,
        grid=(n // tn, m // tm),
        in_specs=[pl.BlockSpec((tm, k), lambda j, i: (i, 0)),
                  pl.BlockSpec((tn, k), lambda j, i: (j, 0))],
        out_specs=pl.BlockSpec((tm, tn), lambda j, i: (i, j)),
        out_shape=jax.ShapeDtypeStruct((m, n), F32),
        compiler_params=pltpu.CompilerParams(
            dimension_semantics=("arbitrary", "arbitrary"),
            vmem_limit_bytes=vmem),
        name="in_proj",
    )(x, w_b)


def _xcast_dt_kernel(x_ref, w_ref, xb_ref, dt_ref, wb_s):
    @pl.when(pl.program_id(0) == 0)
    def _():
        wb_s[...] = w_ref[...].T.astype(BF16)

    xb = x_ref[...].astype(BF16)
    xb_ref[...] = xb
    dt_ref[...] = jnp.dot(xb, wb_s[...], preferred_element_type=F32)


def _xcast_dt(x2d, w_t, *, layer, row0, tm):
    m, k = x2d.shape
    assert row0 % SUBLANES == 0 and m % tm == 0
    row0 = layer * w_t.shape[1] + row0
    vmem = 2 * (tm * k * 4 + tm * k * 2 + LANES * k * 4 + tm * LANES * 4) + 8 * MIB
    return pl.pallas_call(
        _xcast_dt_kernel,
        grid=(m // tm,),
        in_specs=[pl.BlockSpec((tm, k), lambda i: (i, 0)),
                  pl.BlockSpec((pl.Element(LANES), pl.Element(k)), lambda i: (row0, 0))],
        out_specs=[pl.BlockSpec((tm, k), lambda i: (i, 0)),
                   pl.BlockSpec((tm, LANES), lambda i: (i, 0))],
        out_shape=[jax.ShapeDtypeStruct((m, k), BF16),
                   jax.ShapeDtypeStruct((m, LANES), F32)],
        scratch_shapes=[pltpu.VMEM((k, LANES), BF16)],
        compiler_params=pltpu.CompilerParams(
            dimension_semantics=("arbitrary",), vmem_limit_bytes=vmem),
        name="xcast_dt",
    )(x2d, w_t.reshape(-1, k))


def _ssd_kernel(z_ref, xs_ref, b_ref, c_ref, dt_ref,
                cwx_ref, cwb_ref, cwc_ref, cbx_ref, cbb_ref, cbc_ref,
                dtb_ref, alog_ref, d_ref, nw_ref, wout_ref,
                o_ref, woutb_ref,
                dt_s, acs_s, acst_s, state_s, tail_s, ext_s):
    c = pl.program_id(1)
    gstep = pl.program_id(2)
    L = SSM_CHUNK
    gs = SSD_GROUPS_PER_STEP

    woutb_ref[...] = wout_ref[...].astype(BF16)
    gw = xs_ref.shape[1] // gs

    row_id = lax.broadcasted_iota(jnp.int32, (L, LANES), 0)

    @pl.when(gstep == 0)
    def _():
        dt = _softplus(dt_ref[...] + dtb_ref[...])
        a_dt = dt * (-jnp.exp(alog_ref[...]))
        acs = a_dt
        k = 1
        while k < L:
            acs = acs + jnp.where(row_id >= k, pltpu.roll(acs, k, axis=0), 0.0)
            k *= 2
        acs2 = acs * LOG2E
        dt_s[...] = dt
        acs_s[...] = acs2
        acst_s[...] = acs2.T

    for gg in range(gs):
        _ssd_group(gg, gstep * gs + gg, c, gw,
                   z_ref, xs_ref, b_ref, c_ref, cwx_ref, cwb_ref, cwc_ref,
                   cbx_ref, cbb_ref, cbc_ref, d_ref, nw_ref, o_ref,
                   dt_s, acs_s, acst_s, state_s, tail_s, ext_s)


def _ssd_group(gg, g, c, gw, z_ref, xs_ref, b_ref, c_ref, cwx_ref, cwb_ref, cwc_ref,
               cbx_ref, cbb_ref, cbc_ref, d_ref, nw_ref, o_ref,
               dt_s, acs_s, acst_s, state_s, tail_s, ext_s):
    L = SSM_CHUNK
    heads_per_group = gw // SSM_HEADDIM
    pairs = gw // LANES
    n_xs = gw // LANES
    x0 = gg * gw
    bc0 = gg * SSM_STATE
    row_id = lax.broadcasted_iota(jnp.int32, (L, LANES), 0)
    lane_id = lax.broadcasted_iota(jnp.int32, (L, LANES), 1)

    @pl.when(c == 0)
    def _():
        state_s[g] = jnp.zeros(state_s.shape[1:], F32)
        tail_s[g] = jnp.zeros(tail_s.shape[1:], F32)

    acts = []
    for s in range(n_xs + 2):
        if s < n_xs:
            lanes = slice(x0 + s * LANES, x0 + (s + 1) * LANES)
            in_ref, w_ref, bias_ref = xs_ref, cwx_ref, cbx_ref
        else:
            lanes = slice(bc0, bc0 + LANES)
            in_ref, w_ref, bias_ref = ((b_ref, cwb_ref, cbb_ref) if s == n_xs
                                       else (c_ref, cwc_ref, cbc_ref))
        raw = in_ref[:, lanes]
        slab = gg * (n_xs + 2) + s
        ext_s[slab, 0:SUBLANES, :] = tail_s[g, :, s * LANES:(s + 1) * LANES]
        ext_s[slab, SUBLANES:SUBLANES + L, :] = raw
        tail_s[g, :, s * LANES:(s + 1) * LANES] = raw[L - SUBLANES:, :]
        acc = raw * w_ref[CONV_WIDTH - 1:CONV_WIDTH, lanes] + bias_ref[:, lanes]
        for k in range(1, CONV_WIDTH):
            acc = acc + (ext_s[slab, pl.ds(SUBLANES - k, L), :]
                         * w_ref[CONV_WIDTH - 1 - k:CONV_WIDTH - k, lanes])
        half = 0.5 * acc
        acts.append(half + half * jnp.tanh(half))
    xs = jnp.concatenate(acts[:n_xs], axis=1)
    bm = acts[n_xs].astype(BF16)
    cm = acts[n_xs + 1].astype(BF16)

    cb_mat = lax.dot_general(cm, bm, (((1,), (1,)), ((), ())),
                             preferred_element_type=F32)
    s_prev = state_s[g]
    y_off = jnp.dot(cm, s_prev.astype(BF16), preferred_element_type=F32)

    dt_all = dt_s[...]
    acs_all = acs_s[...]
    causal = row_id >= lane_id
    low = lane_id < SSM_HEADDIM

    def head_col(arr, idx):
        return jnp.sum(jnp.where(lane_id == idx, arr, 0.0), axis=1, keepdims=True)

    def head_pair(arr, idx0):
        return jnp.where(low, head_col(arr, idx0), head_col(arr, idx0 + 1))

    y_parts, xdec_parts, cdec_parts = [], [], []
    for j in range(pairs):
        i0 = g * heads_per_group + 2 * j
        i1 = i0 + 1
        col0 = head_col(acs_all, i0)
        col1 = head_col(acs_all, i1)
        row0 = acst_s[pl.ds(i0, 1), :]
        row1 = acst_s[pl.ds(i1, 1), :]
        l0 = jnp.where(causal, jnp.exp2(col0 - row0), 0.0)
        l1 = jnp.where(causal, jnp.exp2(col1 - row1), 0.0)
        m_pair = jnp.concatenate([cb_mat * l0, cb_mat * l1], axis=1).astype(BF16)

        dt_pair = head_pair(dt_all, i0)
        xs_pair = xs[:, j * LANES:(j + 1) * LANES]
        xt = xs_pair * dt_pair
        xt_b = xt.astype(BF16)
        zero = jnp.zeros_like(xt_b)
        rhs = jnp.concatenate([jnp.where(low, xt_b, zero),
                               jnp.where(low, zero, xt_b)], axis=0)
        y_diag = jnp.dot(m_pair, rhs, preferred_element_type=F32)

        scale = jnp.where(low, jnp.exp2(col0), jnp.exp2(col1))
        last0 = col0[L - 1:L, :]
        last1 = col1[L - 1:L, :]
        dec = jnp.where(low, jnp.exp2(last0 - col0), jnp.exp2(last1 - col1))
        y_parts.append(y_diag + y_off[:, j * LANES:(j + 1) * LANES] * scale
                       + d_ref[:, x0 + j * LANES:x0 + (j + 1) * LANES] * xs_pair)
        xdec_parts.append((xt * dec).astype(BF16))
        cdec_parts.append(scale[L - 1:L, :])

    xdec = jnp.concatenate(xdec_parts, axis=1)
    cdec = jnp.concatenate(cdec_parts, axis=1)
    new_states = lax.dot_general(bm, xdec, (((0,), (0,)), ((), ())),
                                 preferred_element_type=F32)
    state_s[g] = s_prev * cdec + new_states

    y = jnp.concatenate(y_parts, axis=1)
    zh = 0.5 * z_ref[:, x0:x0 + gw]
    y = y * (zh + zh * jnp.tanh(zh))
    ms = jnp.mean(y * y, axis=1, keepdims=True)
    o_ref[:, x0:x0 + gw] = (y * lax.rsqrt(ms + NORM_EPS)
                            * nw_ref[:, x0:x0 + gw]).astype(o_ref.dtype)


def _ssd_mixer(proj_a, dt_raw, conv_w, conv_b, dt_bias, a_log, d_exp, norm_w, w_out,
               *, batch, seq, d_ssm):
    L = SSM_CHUNK
    gs = SSD_GROUPS_PER_STEP
    gw = d_ssm // SSM_GROUPS
    xw = gs * gw
    sw = gs * SSM_STATE
    nchunk = seq // L
    t = batch * seq
    assert SSM_GROUPS % gs == 0
    n_gstep = SSM_GROUPS // gs
    n_step = batch * nchunk * n_gstep
    w_rows, w_cols = w_out.shape[0] // n_step, w_out.shape[1]
    assert w_out.shape[0] % n_step == 0 and w_rows % (2 * SUBLANES) == 0

    def step(b, c, g):
        return (b * nchunk + c) * n_gstep + g

    xs0 = d_ssm // xw
    b0 = 2 * d_ssm // sw
    c0 = b0 + SSM_GROUPS // gs
    cwb0 = d_ssm // sw
    cwc0 = cwb0 + SSM_GROUPS // gs

    def rows(b, c, g):
        return b * nchunk + c

    in_specs = [
        pl.BlockSpec((L, xw), lambda b, c, g: (rows(b, c, g), g)),
        pl.BlockSpec((L, xw), lambda b, c, g: (rows(b, c, g), xs0 + g)),
        pl.BlockSpec((L, sw), lambda b, c, g: (rows(b, c, g), b0 + g)),
        pl.BlockSpec((L, sw), lambda b, c, g: (rows(b, c, g), c0 + g)),
        pl.BlockSpec((L, LANES), lambda b, c, g: (rows(b, c, g), 0)),
        pl.BlockSpec((CONV_WIDTH, xw), lambda b, c, g: (0, g)),
        pl.BlockSpec((CONV_WIDTH, sw), lambda b, c, g: (0, cwb0 + g)),
        pl.BlockSpec((CONV_WIDTH, sw), lambda b, c, g: (0, cwc0 + g)),
        pl.BlockSpec((1, xw), lambda b, c, g: (0, g)),
        pl.BlockSpec((1, sw), lambda b, c, g: (0, cwb0 + g)),
        pl.BlockSpec((1, sw), lambda b, c, g: (0, cwc0 + g)),
        pl.BlockSpec((1, LANES), lambda b, c, g: (0, 0)),
        pl.BlockSpec((1, LANES), lambda b, c, g: (0, 0)),
        pl.BlockSpec((1, xw), lambda b, c, g: (0, g)),
        pl.BlockSpec((1, xw), lambda b, c, g: (0, g)),
        pl.BlockSpec((w_rows, w_cols), lambda b, c, g: (step(b, c, g), 0)),
    ]
    return pl.pallas_call(
        _ssd_kernel,
        grid=(batch, nchunk, n_gstep),
        in_specs=in_specs,
        out_specs=[pl.BlockSpec((L, xw), lambda b, c, g: (rows(b, c, g), g)),
                   pl.BlockSpec((w_rows, w_cols), lambda b, c, g: (step(b, c, g), 0))],
        out_shape=[jax.ShapeDtypeStruct((t, d_ssm), BF16),
                   jax.ShapeDtypeStruct(w_out.shape, BF16)],
        scratch_shapes=[
            pltpu.VMEM((L, LANES), F32),
            pltpu.VMEM((L, LANES), F32),
            pltpu.VMEM((LANES, L), F32),
            pltpu.VMEM((SSM_GROUPS, SSM_STATE, gw), F32),
            pltpu.VMEM((SSM_GROUPS, SUBLANES, gw + 2 * SSM_STATE), F32),
            pltpu.VMEM((gs * (gw // LANES + 2), SUBLANES + L, LANES), F32),
        ],
        compiler_params=pltpu.CompilerParams(
            dimension_semantics=("arbitrary", "arbitrary", "arbitrary"),
            vmem_limit_bytes=32 * MIB + 2 * w_rows * w_cols * 6),
        name="ssd_mixer",
    )(proj_a, proj_a, proj_a, proj_a, dt_raw,
      conv_w, conv_w, conv_w, conv_b, conv_b, conv_b,
      dt_bias, a_log, d_exp, norm_w, w_out)


def _lru_kernel(lx_ref, lg_ref, cw_ref, cb_ref, wa_ref, ba_ref, wx_ref, bx_ref,
                lam_ref, wside_ref, o_ref, wside_out_ref, tail_s, h_s):
    tb = pl.program_id(2)

    wside_out_ref[...] = wside_ref[...].astype(BF16)
    n_slab, ts, _ = lx_ref.shape
    n_group = ts // LRU_GROUP
    slabs_per_head = LRU_BLOCK // LANES
    S = SUBLANES
    vshape = (S, LANES)

    @pl.when(tb == 0)
    def _():
        tail_s[...] = jnp.zeros(tail_s.shape, F32)
        h_s[...] = jnp.zeros(h_s.shape, F32)

    sub = lax.broadcasted_iota(jnp.int32, vshape, 0)
    seg0 = sub == 0

    def seg_rows(ref, jj, row0):
        return ref[jj, pl.ds(row0, S, stride=S), :]

    u = []
    for jj in range(n_slab):
        lanes = slice(jj * LANES, (jj + 1) * LANES)
        taps = [jnp.broadcast_to(cw_ref[k:k + 1, lanes], vshape) for k in range(CONV_WIDTH)]
        bias = jnp.broadcast_to(cb_ref[:, lanes], vshape)
        u_slab = []
        for g in range(n_group):
            g0 = g * LRU_GROUP
            x_t = [seg_rows(lx_ref, jj, g0 + t) for t in range(S)]
            before = []
            for d in range(CONV_WIDTH - 1, 0, -1):
                if g == 0:
                    prev_row = jnp.broadcast_to(tail_s[jj, S - d:S - d + 1, :], vshape)
                    before.append(jnp.where(seg0, prev_row,
                                            pltpu.roll(x_t[S - d], 1, axis=0)))
                else:
                    before.append(seg_rows(lx_ref, jj, g0 - d))
            ext = before + x_t
            u_g = []
            for t in range(S):
                acc = ext[t + 3] * taps[3] + bias
                for k in range(CONV_WIDTH - 1):
                    acc = acc + ext[t + k] * taps[k]
                u_g.append(acc)
            u_slab.append(u_g)
        tail_s[jj] = lx_ref[jj, ts - S:ts, :]
        u.append(u_slab)

    pre_r, pre_i = [], []
    for h in range(n_slab // slabs_per_head):
        lhs = jnp.concatenate(
            [jnp.concatenate([u[h * slabs_per_head + q][g][t] for q in range(slabs_per_head)],
                             axis=1)
             for g in range(n_group) for t in range(S)], axis=0).astype(BF16)
        pre_r.append(jnp.dot(lhs, wa_ref[h], preferred_element_type=F32))
        pre_i.append(jnp.dot(lhs, wx_ref[h], preferred_element_type=F32))

    for jj in range(n_slab):
        lanes = slice(jj * LANES, (jj + 1) * LANES)
        h_idx, q = divmod(jj, slabs_per_head)
        qlanes = slice(q * LANES, (q + 1) * LANES)
        half_coef = jnp.broadcast_to(-0.5 * LRU_C * _softplus(-lam_ref[:, lanes]), vshape)
        b_r = jnp.broadcast_to(ba_ref[:, lanes], vshape)
        b_i = jnp.broadcast_to(bx_ref[:, lanes], vshape)
        carry = h_s[jj]
        for g in range(n_group):
            g0 = g * LRU_GROUP
            a_t, h_t = [], []
            for t in range(S):
                r0 = g0 + t * S
                t_r = jnp.tanh(pre_r[h_idx][r0:r0 + S, qlanes] + b_r)
                t_i = jnp.tanh(pre_i[h_idx][r0:r0 + S, qlanes] + b_i)
                log_a = half_coef * t_r + half_coef
                a = jnp.exp(log_a)
                th = jnp.tanh(log_a)
                q4 = (-0.5 * th) / (1.0 - th)
                root = jnp.where(q4 > 0.0, q4 * lax.rsqrt(q4), 0.0)
                bv = root * ((t_i + 1.0) * u[jj][g][t])
                if t == 0:
                    a_t.append(a)
                    h_t.append(bv)
                else:
                    a_t.append(a * a_t[-1])
                    h_t.append(a * h_t[-1] + bv)
            a_e, h_e = a_t[-1], h_t[-1]
            k = 1
            while k < S:
                keep = sub >= k
                a_sh = jnp.where(keep, pltpu.roll(a_e, k, axis=0), 1.0)
                h_sh = jnp.where(keep, pltpu.roll(h_e, k, axis=0), 0.0)
                h_e = a_e * h_sh + h_e
                a_e = a_e * a_sh
                k *= 2
            ends = h_e + a_e * carry
            seg_in = jnp.where(seg0, carry, pltpu.roll(ends, 1, axis=0))
            carry = jnp.broadcast_to(ends[S - 1:S, :], vshape)
            for t in range(S):
                hv = h_t[t] + a_t[t] * seg_in
                lh = 0.5 * seg_rows(lg_ref, jj, g0 + t)
                silu = lh + lh * jnp.tanh(lh)
                o_ref[jj, pl.ds(g0 + t, S, stride=S), :] = hv * silu
        h_s[jj] = carry


def _lru_mixer(proj_l, conv_w, conv_b, wa, ba, wx, bx, lam, w_flat, *, side_row0, side_rows,
               batch, seq, d_lru, ts, cbw):
    t = batch * seq
    nt = seq // ts
    ncb = d_lru // cbw
    n_slab = cbw // LANES
    hb = cbw // LRU_BLOCK
    n_step = batch * ncb * nt
    k = w_flat.shape[1]
    sr = side_rows // n_step
    assert ts % LRU_GROUP == 0 and seq % ts == 0
    assert side_rows % n_step == 0 and sr % (2 * SUBLANES) == 0 and side_row0 % SUBLANES == 0

    def step(b, j, s):
        return (b * ncb + j) * nt + s

    vec = pl.BlockSpec((1, cbw), lambda b, j, s: (0, j))
    return pl.pallas_call(
        _lru_kernel,
        grid=(batch, ncb, nt),
        in_specs=[
            pl.BlockSpec((n_slab, ts, LANES), lambda b, j, s: (j, b * nt + s, 0)),
            pl.BlockSpec((n_slab, ts, LANES), lambda b, j, s: (ncb + j, b * nt + s, 0)),
            pl.BlockSpec((CONV_WIDTH, cbw), lambda b, j, s: (0, j)),
            vec,
            pl.BlockSpec((hb, LRU_BLOCK, LRU_BLOCK), lambda b, j, s: (j, 0, 0)),
            vec,
            pl.BlockSpec((hb, LRU_BLOCK, LRU_BLOCK), lambda b, j, s: (j, 0, 0)),
            vec,
            vec,
            pl.BlockSpec((pl.Element(sr), pl.Element(k)),
                         lambda b, j, s: (pl.multiple_of(side_row0 + step(b, j, s) * sr,
                                                         SUBLANES), 0)),
        ],
        out_specs=[pl.BlockSpec((n_slab, ts, LANES), lambda b, j, s: (j, b * nt + s, 0)),
                   pl.BlockSpec((sr, k), lambda b, j, s: (step(b, j, s), 0))],
        out_shape=[jax.ShapeDtypeStruct((d_lru // LANES, t, LANES), F32),
                   jax.ShapeDtypeStruct((side_rows, k), BF16)],
        scratch_shapes=[pltpu.VMEM((n_slab, SUBLANES, LANES), F32),
                        pltpu.VMEM((n_slab, SUBLANES, LANES), F32)],
        compiler_params=pltpu.CompilerParams(
            dimension_semantics=("arbitrary", "arbitrary", "arbitrary"),
            vmem_limit_bytes=32 * MIB + 2 * sr * k * 6),
        name="lru_mixer",
    )(proj_l, proj_l, conv_w, conv_b, wa, ba, wx, bx, lam, w_flat)


def _out_kernel(ssd_ref, lru_ref, w_ref, x_hbm, g_ref, b_ref, o_hbm, acc_s, xch_s,
                xsem, osem, *, nk_half, alpha):
    i = pl.program_id(0)
    k = pl.program_id(1)
    tm, d_model = acc_s.shape
    n_chunk = tm // LN_ROWS
    n_slot = xch_s.shape[0]

    def x_copy(c, slot):
        rows = pl.ds(pl.multiple_of(i * tm + c * LN_ROWS, LN_ROWS), LN_ROWS)
        return pltpu.make_async_copy(x_hbm.at[rows, :], xch_s.at[slot], xsem.at[slot])

    def o_copy(c):
        src = pl.ds(pl.multiple_of(c * LN_ROWS, LN_ROWS), LN_ROWS)
        dst = pl.ds(pl.multiple_of(i * tm + c * LN_ROWS, LN_ROWS), LN_ROWS)
        return pltpu.make_async_copy(acc_s.at[src, :], o_hbm.at[dst, :], osem)

    def accumulate(lhs, first):
        for n in range(0, d_model, OUT_N_CHUNK):
            sl = slice(n, n + OUT_N_CHUNK)
            part = jnp.dot(lhs, w_ref[:, sl], preferred_element_type=F32)
            if first:
                acc_s[:, sl] = part
            else:
                acc_s[:, sl] += part

    @pl.when(k == 0)
    def _():
        for c in range(n_slot):
            x_copy(c, c).start()
        accumulate(ssd_ref[...], True)

    @pl.when(jnp.logical_and(k > 0, k < nk_half))
    def _():
        accumulate(ssd_ref[...], False)

    @pl.when(k >= nk_half)
    def _():
        lhs = jnp.concatenate([lru_ref[jj] for jj in range(lru_ref.shape[0])], axis=1)
        accumulate(lhs.astype(BF16), False)

    @pl.when(k == 2 * nk_half - 1)
    def _():
        def ln_chunk(c, carry):
            slot = c % n_slot
            x_copy(c, slot).wait()
            rows = pl.ds(pl.multiple_of(c * LN_ROWS, LN_ROWS), LN_ROWS)
            res = acc_s[rows, :] + alpha * xch_s[slot]
            mu = jnp.mean(res, axis=1, keepdims=True)
            cen = res - mu
            var = jnp.mean(cen * cen, axis=1, keepdims=True)
            acc_s[rows, :] = cen * lax.rsqrt(var + NORM_EPS) * g_ref[...] + b_ref[...]
            o_copy(c).start()

            @pl.when(c + n_slot < n_chunk)
            def _():
                x_copy(c + n_slot, slot).start()

            return carry

        lax.fori_loop(0, n_chunk, ln_chunk, 0)

        def drain(c, carry):
            o_copy(c).wait()
            return carry

        lax.fori_loop(0, n_chunk, drain, 0)


def _out_proj(ssd_out, lru_out, w_out, x2d, ln_g, ln_b, *, alpha, tm, tk):
    t, d_half = ssd_out.shape
    d_model = x2d.shape[1]
    nk_half = d_half // tk
    ks = tk // LANES
    n_slot = min(OUT_X_SLOTS, tm // LN_ROWS)
    vmem = (2 * (tm * tk * 2 + tm * tk * 4 + tk * d_model * 2) + tm * d_model * 4
            + n_slot * LN_ROWS * d_model * 4 + tm * tk * 2 + tm * OUT_N_CHUNK * 4 + 6 * MIB)
    return pl.pallas_call(
        functools.partial(_out_kernel, nk_half=nk_half, alpha=alpha),
        grid=(t // tm, 2 * nk_half),
        in_specs=[
            pl.BlockSpec((tm, tk), lambda i, k: (i, jnp.minimum(k, nk_half - 1))),
            pl.BlockSpec((ks, tm, LANES), lambda i, k: (jnp.maximum(k - nk_half, 0), i, 0)),
            pl.BlockSpec((tk, d_model), lambda i, k: (k, 0)),
            pl.BlockSpec(memory_space=pl.ANY),
            pl.BlockSpec((1, d_model), lambda i, k: (0, 0)),
            pl.BlockSpec((1, d_model), lambda i, k: (0, 0)),
        ],
        out_specs=pl.BlockSpec(memory_space=pl.ANY),
        out_shape=jax.ShapeDtypeStruct((t, d_model), F32),
        scratch_shapes=[pltpu.VMEM((tm, d_model), F32),
                        pltpu.VMEM((n_slot, LN_ROWS, d_model), F32),
                        pltpu.SemaphoreType.DMA((n_slot,)),
                        pltpu.SemaphoreType.DMA(())],
        compiler_params=pltpu.CompilerParams(
            dimension_semantics=("arbitrary", "arbitrary"),
            vmem_limit_bytes=vmem),
        name="out_proj_ln",
    )(ssd_out, lru_out, w_out, x2d, ln_g, ln_b)


def _layer(x2d, layer, w_in, ssd_conv_w, ssd_conv_b, ssd_dt_bias, ssd_a_log, ssd_d,
           ssd_norm_w, lru_conv_w, lru_conv_b, lru_wa, lru_ba, lru_wx, lru_bx, lru_lambda,
           w_out, ln_g, ln_b, *, batch, seq, alpha):
    t, d_model = x2d.shape
    d_ssm = d_model
    d_lru = d_model
    heads = d_ssm // SSM_HEADDIM
    d_xbc = d_ssm + 2 * SSM_GROUPS * SSM_STATE
    n_a = d_ssm + d_xbc
    assert heads <= LANES and seq % SSM_CHUNK == 0

    w_t = jnp.swapaxes(w_in, 1, 2)

    tm = min(1024, t)
    xb, dt_raw = _xcast_dt(x2d, w_t, layer=layer, row0=n_a, tm=min(512, t))

    proj_l = _matmul_staged_slab(xb, w_t, layer=layer, row0=n_a + heads, n_cols=2 * d_lru,
                                 tm=tm, tn=512)
    lru_out, w_a = _lru_mixer(
        proj_l, lru_conv_w, lru_conv_b.reshape(1, d_lru),
        (0.5 * lru_wa).astype(BF16), 0.5 * lru_ba.reshape(1, d_lru),
        (0.5 * lru_wx).astype(BF16), 0.5 * lru_bx.reshape(1, d_lru),
        lru_lambda.reshape(1, d_lru), w_t.reshape(-1, d_model),
        side_row0=layer * w_t.shape[1], side_rows=n_a,
        batch=batch, seq=seq, d_lru=d_lru, ts=min(1024, seq), cbw=512)

    proj_a = _matmul_bf16(xb, w_a, tm=tm, tn=1024)
    pad_h = (0, LANES - heads)
    ssd_out, w_out_b = _ssd_mixer(
        proj_a, dt_raw, ssd_conv_w, ssd_conv_b.reshape(1, d_xbc),
        jnp.pad(ssd_dt_bias, pad_h).reshape(1, LANES),
        jnp.pad(ssd_a_log, pad_h).reshape(1, LANES),
        jnp.repeat(ssd_d, SSM_HEADDIM).reshape(1, d_ssm),
        ssd_norm_w.reshape(1, d_ssm), w_out,
        batch=batch, seq=seq, d_ssm=d_ssm)

    return _out_proj(ssd_out, lru_out, w_out_b, x2d,
                     ln_g.reshape(1, d_model), ln_b.reshape(1, d_model),
                     alpha=alpha, tm=min(1024, t), tk=1024)


def kernel(x, w_in, ssd_conv_w, ssd_conv_b, ssd_dt_bias, ssd_a_log, ssd_d, ssd_norm_w,
           lru_conv_w, lru_conv_b, lru_wa, lru_ba, lru_wx, lru_bx, lru_lambda,
           w_out, ln_g, ln_b):
    batch, seq, d_model = x.shape
    depth = w_in.shape[0]
    alpha = (2.0 * depth) ** 0.25
    h = x.reshape(batch * seq, d_model)
    for layer in range(depth):
        h = _layer(h, layer, w_in, ssd_conv_w[layer], ssd_conv_b[layer],
                   ssd_dt_bias[layer], ssd_a_log[layer], ssd_d[layer], ssd_norm_w[layer],
                   lru_conv_w[layer], lru_conv_b[layer], lru_wa[layer], lru_ba[layer],
                   lru_wx[layer], lru_bx[layer], lru_lambda[layer],
                   w_out[layer], ln_g[layer], ln_b[layer],
                   batch=batch, seq=seq, alpha=alpha)
    return h.reshape(batch, seq, d_model)
```

```python
import functools

import jax
import jax.numpy as jnp
from jax import lax
from jax.experimental import pallas as pl
from jax.experimental.pallas import tpu as pltpu

F32 = jnp.float32
BF16 = jnp.bfloat16

SSM_HEADDIM = 64
SSM_GROUPS = 8
SSM_STATE = 128
SSM_CHUNK = 128
LRU_BLOCK = 256
LRU_C = 8.0
CONV_WIDTH = 4
NORM_EPS = 1e-5
LOG2E = 1.4426950408889634

LANES = 128
SUBLANES = 8
MIB = 1024 * 1024
OUT_N_CHUNK = 512
LN_ROWS = 64
OUT_X_SLOTS = 8
LRU_GROUP = SUBLANES * SUBLANES
SSD_GROUPS_PER_STEP = 8

def _softplus(v):
    return jnp.maximum(v, 0.0) + jnp.log1p(jnp.exp(-jnp.abs(v)))


def _mm_staged_kernel(x_ref, w_ref, o_ref, wb_s):
    @pl.when(pl.program_id(1) == 0)
    def _():
        for r in range(0, w_ref.shape[0], LANES):
            wb_s[:, r:r + LANES] = w_ref[r:r + LANES, :].T.astype(BF16)

    res = jnp.dot(x_ref[...], wb_s[...], preferred_element_type=F32)
    for jj in range(o_ref.shape[0]):
        o_ref[jj] = res[:, jj * LANES:(jj + 1) * LANES]


def _matmul_staged_slab(x, w_t, *, layer, row0, n_cols, tm, tn):
    m, k = x.shape
    assert row0 % SUBLANES == 0 and n_cols % tn == 0 and m % tm == 0 and tn % LANES == 0
    row0 = layer * w_t.shape[1] + row0
    w_spec = pl.BlockSpec((pl.Element(tn), pl.Element(k)),
                          lambda j, i: (pl.multiple_of(row0 + j * tn, SUBLANES), 0))
    vmem = (2 * (tm * k * 2 + tn * k * 4 + tm * tn * 4) + tn * k * 2 + tm * tn * 4
            + 6 * MIB)
    return pl.pallas_call(
        _mm_staged_kernel,
        grid=(n_cols // tn, m // tm),
        in_specs=[pl.BlockSpec((tm, k), lambda j, i: (i, 0)), w_spec],
        out_specs=pl.BlockSpec((tn // LANES, tm, LANES), lambda j, i: (j, i, 0)),
        out_shape=jax.ShapeDtypeStruct((n_cols // LANES, m, LANES), F32),
        scratch_shapes=[pltpu.VMEM((k, tn), BF16)],
        compiler_params=pltpu.CompilerParams(
            dimension_semantics=("arbitrary", "arbitrary"),
            vmem_limit_bytes=vmem),
        name="in_proj_lru",
    )(x, w_t.reshape(-1, k))


def _mm_bf16_kernel(x_ref, w_ref, o_ref):
    o_ref[...] = lax.dot_general(x_ref[...], w_ref[...], (((1,), (1,)), ((), ())),
                                 preferred_element_type=F32)


def _matmul_bf16(x, w_b, *, tm, tn):
    m, k = x.shape
    n = w_b.shape[0]
    assert n % tn == 0 and m % tm == 0 and tn % LANES == 0
    vmem = 2 * (tm * k * 2 + tn * k * 2 + tm * tn * 4) + tm * tn * 4 + 6 * MIB
    return pl.pallas_call(
        _mm_bf16_kernel,
        grid=(n // tn, m // tm),
        in_specs=[pl.BlockSpec((tm, k), lambda j, i: (i, 0)),
                  pl.BlockSpec((tn, k), lambda j, i: (j, 0))],
        out_specs=pl.BlockSpec((tm, tn), lambda j, i: (i, j)),
        out_shape=jax.ShapeDtypeStruct((m, n), F32),
        compiler_params=pltpu.CompilerParams(
            dimension_semantics=("arbitrary", "arbitrary"),
            vmem_limit_bytes=vmem),
        name="in_proj",
    )(x, w_b)


def _xcast_dt_kernel(x_ref, w_ref, xb_ref, dt_ref, wb_s):
    @pl.when(pl.program_id(0) == 0)
    def _():
        wb_s[...] = w_ref[...].T.astype(BF16)

    xb = x_ref[...].astype(BF16)
    xb_ref[...] = xb
    dt_ref[...] = jnp.dot(xb, wb_s[...], preferred_element_type=F32)


def _xcast_dt(x2d, w_t, *, layer, row0, tm):
    m, k = x2d.shape
    assert row0 % SUBLANES == 0 and m % tm == 0
    row0 = layer * w_t.shape[1] + row0
    vmem = 2 * (tm * k * 4 + tm * k * 2 + LANES * k * 4 + tm * LANES * 4) + 8 * MIB
    return pl.pallas_call(
        _xcast_dt_kernel,
        grid=(m // tm,),
        in_specs=[pl.BlockSpec((tm, k), lambda i: (i, 0)),
                  pl.BlockSpec((pl.Element(LANES), pl.Element(k)), lambda i: (row0, 0))],
        out_specs=[pl.BlockSpec((tm, k), lambda i: (i, 0)),
                   pl.BlockSpec((tm, LANES), lambda i: (i, 0))],
        out_shape=[jax.ShapeDtypeStruct((m, k), BF16),
                   jax.ShapeDtypeStruct((m, LANES), F32)],
        scratch_shapes=[pltpu.VMEM((k, LANES), BF16)],
        compiler_params=pltpu.CompilerParams(
            dimension_semantics=("arbitrary",), vmem_limit_bytes=vmem),
        name="xcast_dt",
    )(x2d, w_t.reshape(-1, k))


def _ssd_kernel(z_ref, xs_ref, b_ref, c_ref, dt_ref,
                cwx_ref, cwb_ref, cwc_ref, cbx_ref, cbb_ref, cbc_ref,
                dtb_ref, alog_ref, d_ref, nw_ref, wout_ref,
                o_ref, woutb_ref,
                dt_s, acs_s, acst_s, state_s, tail_s, ext_s):
    c = pl.program_id(1)
    gstep = pl.program_id(2)
    L = SSM_CHUNK
    gs = SSD_GROUPS_PER_STEP

    woutb_ref[...] = wout_ref[...].astype(BF16)
    gw = xs_ref.shape[1] // gs

    row_id = lax.broadcasted_iota(jnp.int32, (L, LANES), 0)

    @pl.when(gstep == 0)
    def _():
        dt = _softplus(dt_ref[...] + dtb_ref[...])
        a_dt = dt * (-jnp.exp(alog_ref[...]))
        acs = a_dt
        k = 1
        while k < L:
            acs = acs + jnp.where(row_id >= k, pltpu.roll(acs, k, axis=0), 0.0)
            k *= 2
        acs2 = acs * LOG2E
        dt_s[...] = dt
        acs_s[...] = acs2
        acst_s[...] = acs2.T

    for gg in range(gs):
        _ssd_group(gg, gstep * gs + gg, c, gw,
                   z_ref, xs_ref, b_ref, c_ref, cwx_ref, cwb_ref, cwc_ref,
                   cbx_ref, cbb_ref, cbc_ref, d_ref, nw_ref, o_ref,
                   dt_s, acs_s, acst_s, state_s, tail_s, ext_s)


def _ssd_group(gg, g, c, gw, z_ref, xs_ref, b_ref, c_ref, cwx_ref, cwb_ref, cwc_ref,
               cbx_ref, cbb_ref, cbc_ref, d_ref, nw_ref, o_ref,
               dt_s, acs_s, acst_s, state_s, tail_s, ext_s):
    L = SSM_CHUNK
    heads_per_group = gw // SSM_HEADDIM
    pairs = gw // LANES
    n_xs = gw // LANES
    x0 = gg * gw
    bc0 = gg * SSM_STATE
    row_id = lax.broadcasted_iota(jnp.int32, (L, LANES), 0)
    lane_id = lax.broadcasted_iota(jnp.int32, (L, LANES), 1)

    @pl.when(c == 0)
    def _():
        state_s[g] = jnp.zeros(state_s.shape[1:], F32)
        tail_s[g] = jnp.zeros(tail_s.shape[1:], F32)

    acts = []
    for s in range(n_xs + 2):
        if s < n_xs:
            lanes = slice(x0 + s * LANES, x0 + (s + 1) * LANES)
            in_ref, w_ref, bias_ref = xs_ref, cwx_ref, cbx_ref
        else:
            lanes = slice(bc0, bc0 + LANES)
            in_ref, w_ref, bias_ref = ((b_ref, cwb_ref, cbb_ref) if s == n_xs
                                       else (c_ref, cwc_ref, cbc_ref))
        raw = in_ref[:, lanes]
        slab = gg * (n_xs + 2) + s
        ext_s[slab, 0:SUBLANES, :] = tail_s[g, :, s * LANES:(s + 1) * LANES]
        ext_s[slab, SUBLANES:SUBLANES + L, :] = raw
        tail_s[g, :, s * LANES:(s + 1) * LANES] = raw[L - SUBLANES:, :]
        acc = raw * w_ref[CONV_WIDTH - 1:CONV_WIDTH, lanes] + bias_ref[:, lanes]
        for k in range(1, CONV_WIDTH):
            acc = acc + (ext_s[slab, pl.ds(SUBLANES - k, L), :]
                         * w_ref[CONV_WIDTH - 1 - k:CONV_WIDTH - k, lanes])
        half = 0.5 * acc
        acts.append(half + half * jnp.tanh(half))
    xs = jnp.concatenate(acts[:n_xs], axis=1)
    bm = acts[n_xs].astype(BF16)
    cm = acts[n_xs + 1].astype(BF16)

    cb_mat = lax.dot_general(cm, bm, (((1,), (1,)), ((), ())),
                             preferred_element_type=F32)
    s_prev = state_s[g]
    y_off = jnp.dot(cm, s_prev.astype(BF16), preferred_element_type=F32)

    dt_all = dt_s[...]
    acs_all = acs_s[...]
    causal = row_id >= lane_id
    low = lane_id < SSM_HEADDIM

    def head_col(arr, idx):
        return jnp.sum(jnp.where(lane_id == idx, arr, 0.0), axis=1, keepdims=True)

    def head_pair(arr, idx0):
        return jnp.where(low, head_col(arr, idx0), head_col(arr, idx0 + 1))

    y_parts, xdec_parts, cdec_parts = [], [], []
    for j in range(pairs):
        i0 = g * heads_per_group + 2 * j
        i1 = i0 + 1
        col0 = head_col(acs_all, i0)
        col1 = head_col(acs_all, i1)
        row0 = acst_s[pl.ds(i0, 1), :]
        row1 = acst_s[pl.ds(i1, 1), :]
        l0 = jnp.where(causal, jnp.exp2(col0 - row0), 0.0)
        l1 = jnp.where(causal, jnp.exp2(col1 - row1), 0.0)
        m_pair = jnp.concatenate([cb_mat * l0, cb_mat * l1], axis=1).astype(BF16)

        dt_pair = head_pair(dt_all, i0)
        xs_pair = xs[:, j * LANES:(j + 1) * LANES]
        xt = xs_pair * dt_pair
        xt_b = xt.astype(BF16)
        zero = jnp.zeros_like(xt_b)
        rhs = jnp.concatenate([jnp.where(low, xt_b, zero),
                               jnp.where(low, zero, xt_b)], axis=0)
        y_diag = jnp.dot(m_pair, rhs, preferred_element_type=F32)

        scale = jnp.where(low, jnp.exp2(col0), jnp.exp2(col1))
        last0 = col0[L - 1:L, :]
        last1 = col1[L - 1:L, :]
        dec = jnp.where(low, jnp.exp2(last0 - col0), jnp.exp2(last1 - col1))
        y_parts.append(y_diag + y_off[:, j * LANES:(j + 1) * LANES] * scale
                       + d_ref[:, x0 + j * LANES:x0 + (j + 1) * LANES] * xs_pair)
        xdec_parts.append((xt * dec).astype(BF16))
        cdec_parts.append(scale[L - 1:L, :])

    xdec = jnp.concatenate(xdec_parts, axis=1)
    cdec = jnp.concatenate(cdec_parts, axis=1)
    new_states = lax.dot_general(bm, xdec, (((0,), (0,)), ((), ())),
                                 preferred_element_type=F32)
    state_s[g] = s_prev * cdec + new_states

    y = jnp.concatenate(y_parts, axis=1)
    zh = 0.5 * z_ref[:, x0:x0 + gw]
    y = y * (zh + zh * jnp.tanh(zh))
    ms = jnp.mean(y * y, axis=1, keepdims=True)
    o_ref[:, x0:x0 + gw] = (y * lax.rsqrt(ms + NORM_EPS)
                            * nw_ref[:, x0:x0 + gw]).astype(o_ref.dtype)


def _ssd_mixer(proj_a, dt_raw, conv_w, conv_b, dt_bias, a_log, d_exp, norm_w, w_out,
               *, batch, seq, d_ssm):
    L = SSM_CHUNK
    gs = SSD_GROUPS_PER_STEP
    gw = d_ssm // SSM_GROUPS
    xw = gs * gw
    sw = gs * SSM_STATE
    nchunk = seq // L
    t = batch * seq
    assert SSM_GROUPS % gs == 0
    n_gstep = SSM_GROUPS // gs
    n_step = batch * nchunk * n_gstep
    w_rows, w_cols = w_out.shape[0] // n_step, w_out.shape[1]
    assert w_out.shape[0] % n_step == 0 and w_rows % (2 * SUBLANES) == 0

    def step(b, c, g):
        return (b * nchunk + c) * n_gstep + g

    xs0 = d_ssm // xw
    b0 = 2 * d_ssm // sw
    c0 = b0 + SSM_GROUPS // gs
    cwb0 = d_ssm // sw
    cwc0 = cwb0 + SSM_GROUPS // gs

    def rows(b, c, g):
        return b * nchunk + c

    in_specs = [
        pl.BlockSpec((L, xw), lambda b, c, g: (rows(b, c, g), g)),
        pl.BlockSpec((L, xw), lambda b, c, g: (rows(b, c, g), xs0 + g)),
        pl.BlockSpec((L, sw), lambda b, c, g: (rows(b, c, g), b0 + g)),
        pl.BlockSpec((L, sw), lambda b, c, g: (rows(b, c, g), c0 + g)),
        pl.BlockSpec((L, LANES), lambda b, c, g: (rows(b, c, g), 0)),
        pl.BlockSpec((CONV_WIDTH, xw), lambda b, c, g: (0, g)),
        pl.BlockSpec((CONV_WIDTH, sw), lambda b, c, g: (0, cwb0 + g)),
        pl.BlockSpec((CONV_WIDTH, sw), lambda b, c, g: (0, cwc0 + g)),
        pl.BlockSpec((1, xw), lambda b, c, g: (0, g)),
        pl.BlockSpec((1, sw), lambda b, c, g: (0, cwb0 + g)),
        pl.BlockSpec((1, sw), lambda b, c, g: (0, cwc0 + g)),
        pl.BlockSpec((1, LANES), lambda b, c, g: (0, 0)),
        pl.BlockSpec((1, LANES), lambda b, c, g: (0, 0)),
        pl.BlockSpec((1, xw), lambda b, c, g: (0, g)),
        pl.BlockSpec((1, xw), lambda b, c, g: (0, g)),
        pl.BlockSpec((w_rows, w_cols), lambda b, c, g: (step(b, c, g), 0)),
    ]
    return pl.pallas_call(
        _ssd_kernel,
        grid=(batch, nchunk, n_gstep),
        in_specs=in_specs,
        out_specs=[pl.BlockSpec((L, xw), lambda b, c, g: (rows(b, c, g), g)),
                   pl.BlockSpec((w_rows, w_cols), lambda b, c, g: (step(b, c, g), 0))],
        out_shape=[jax.ShapeDtypeStruct((t, d_ssm), BF16),
                   jax.ShapeDtypeStruct(w_out.shape, BF16)],
        scratch_shapes=[
            pltpu.VMEM((L, LANES), F32),
            pltpu.VMEM((L, LANES), F32),
            pltpu.VMEM((LANES, L), F32),
            pltpu.VMEM((SSM_GROUPS, SSM_STATE, gw), F32),
            pltpu.VMEM((SSM_GROUPS, SUBLANES, gw + 2 * SSM_STATE), F32),
            pltpu.VMEM((gs * (gw // LANES + 2), SUBLANES + L, LANES), F32),
        ],
        compiler_params=pltpu.CompilerParams(
            dimension_semantics=("arbitrary", "arbitrary", "arbitrary"),
            vmem_limit_bytes=32 * MIB + 2 * w_rows * w_cols * 6),
        name="ssd_mixer",
    )(proj_a, proj_a, proj_a, proj_a, dt_raw,
      conv_w, conv_w, conv_w, conv_b, conv_b, conv_b,
      dt_bias, a_log, d_exp, norm_w, w_out)


def _lru_kernel(lx_ref, lg_ref, cw_ref, cb_ref, wa_ref, ba_ref, wx_ref, bx_ref,
                lam_ref, wside_ref, o_ref, wside_out_ref, tail_s, h_s):
    tb = pl.program_id(2)

    wside_out_ref[...] = wside_ref[...].astype(BF16)
    n_slab, ts, _ = lx_ref.shape
    n_group = ts // LRU_GROUP
    slabs_per_head = LRU_BLOCK // LANES
    S = SUBLANES
    vshape = (S, LANES)

    @pl.when(tb == 0)
    def _():
        tail_s[...] = jnp.zeros(tail_s.shape, F32)
        h_s[...] = jnp.zeros(h_s.shape, F32)

    sub = lax.broadcasted_iota(jnp.int32, vshape, 0)
    seg0 = sub == 0

    def seg_rows(ref, jj, row0):
        return ref[jj, pl.ds(row0, S, stride=S), :]

    u = []
    for jj in range(n_slab):
        lanes = slice(jj * LANES, (jj + 1) * LANES)
        taps = [jnp.broadcast_to(cw_ref[k:k + 1, lanes], vshape) for k in range(CONV_WIDTH)]
        bias = jnp.broadcast_to(cb_ref[:, lanes], vshape)
        u_slab = []
        for g in range(n_group):
            g0 = g * LRU_GROUP
            x_t = [seg_rows(lx_ref, jj, g0 + t) for t in range(S)]
            before = []
            for d in range(CONV_WIDTH - 1, 0, -1):
                if g == 0:
                    prev_row = jnp.broadcast_to(tail_s[jj, S - d:S - d + 1, :], vshape)
                    before.append(jnp.where(seg0, prev_row,
                                            pltpu.roll(x_t[S - d], 1, axis=0)))
                else:
                    before.append(seg_rows(lx_ref, jj, g0 - d))
            ext = before + x_t
            u_g = []
            for t in range(S):
                acc = ext[t + 3] * taps[3] + bias
                for k in range(CONV_WIDTH - 1):
                    acc = acc + ext[t + k] * taps[k]
                u_g.append(acc)
            u_slab.append(u_g)
        tail_s[jj] = lx_ref[jj, ts - S:ts, :]
        u.append(u_slab)

    pre_r, pre_i = [], []
    for h in range(n_slab // slabs_per_head):
        lhs = jnp.concatenate(
            [jnp.concatenate([u[h * slabs_per_head + q][g][t] for q in range(slabs_per_head)],
                             axis=1)
             for g in range(n_group) for t in range(S)], axis=0).astype(BF16)
        pre_r.append(jnp.dot(lhs, wa_ref[h], preferred_element_type=F32))
        pre_i.append(jnp.dot(lhs, wx_ref[h], preferred_element_type=F32))

    for jj in range(n_slab):
        lanes = slice(jj * LANES, (jj + 1) * LANES)
        h_idx, q = divmod(jj, slabs_per_head)
        qlanes = slice(q * LANES, (q + 1) * LANES)
        half_coef = jnp.broadcast_to(-0.5 * LRU_C * _softplus(-lam_ref[:, lanes]), vshape)
        b_r = jnp.broadcast_to(ba_ref[:, lanes], vshape)
        b_i = jnp.broadcast_to(bx_ref[:, lanes], vshape)
        carry = h_s[jj]
        for g in range(n_group):
            g0 = g * LRU_GROUP
            a_t, h_t = [], []
            for t in range(S):
                r0 = g0 + t * S
                t_r = jnp.tanh(pre_r[h_idx][r0:r0 + S, qlanes] + b_r)
                t_i = jnp.tanh(pre_i[h_idx][r0:r0 + S, qlanes] + b_i)
                log_a = half_coef * t_r + half_coef
                a = jnp.exp(log_a)
                th = jnp.tanh(log_a)
                q4 = (-0.5 * th) / (1.0 - th)
                root = jnp.where(q4 > 0.0, q4 * lax.rsqrt(q4), 0.0)
                bv = root * ((t_i + 1.0) * u[jj][g][t])
                if t == 0:
                    a_t.append(a)
                    h_t.append(bv)
                else:
                    a_t.append(a * a_t[-1])
                    h_t.append(a * h_t[-1] + bv)
            a_e, h_e = a_t[-1], h_t[-1]
            k = 1
            while k < S:
                keep = sub >= k
                a_sh = jnp.where(keep, pltpu.roll(a_e, k, axis=0), 1.0)
                h_sh = jnp.where(keep, pltpu.roll(h_e, k, axis=0), 0.0)
                h_e = a_e * h_sh + h_e
                a_e = a_e * a_sh
                k *= 2
            ends = h_e + a_e * carry
            seg_in = jnp.where(seg0, carry, pltpu.roll(ends, 1, axis=0))
            carry = jnp.broadcast_to(ends[S - 1:S, :], vshape)
            for t in range(S):
                hv = h_t[t] + a_t[t] * seg_in
                lh = 0.5 * seg_rows(lg_ref, jj, g0 + t)
                silu = lh + lh * jnp.tanh(lh)
                o_ref[jj, pl.ds(g0 + t, S, stride=S), :] = hv * silu
        h_s[jj] = carry


def _lru_mixer(proj_l, conv_w, conv_b, wa, ba, wx, bx, lam, w_flat, *, side_row0, side_rows,
               batch, seq, d_lru, ts, cbw):
    t = batch * seq
    nt = seq // ts
    ncb = d_lru // cbw
    n_slab = cbw // LANES
    hb = cbw // LRU_BLOCK
    n_step = batch * ncb * nt
    k = w_flat.shape[1]
    sr = side_rows // n_step
    assert ts % LRU_GROUP == 0 and seq % ts == 0
    assert side_rows % n_step == 0 and sr % (2 * SUBLANES) == 0 and side_row0 % SUBLANES == 0

    def step(b, j, s):
        return (b * ncb + j) * nt + s

    vec = pl.BlockSpec((1, cbw), lambda b, j, s: (0, j))
    return pl.pallas_call(
        _lru_kernel,
        grid=(batch, ncb, nt),
        in_specs=[
            pl.BlockSpec((n_slab, ts, LANES), lambda b, j, s: (j, b * nt + s, 0)),
            pl.BlockSpec((n_slab, ts, LANES), lambda b, j, s: (ncb + j, b * nt + s, 0)),
            pl.BlockSpec((CONV_WIDTH, cbw), lambda b, j, s: (0, j)),
            vec,
            pl.BlockSpec((hb, LRU_BLOCK, LRU_BLOCK), lambda b, j, s: (j, 0, 0)),
            vec,
            pl.BlockSpec((hb, LRU_BLOCK, LRU_BLOCK), lambda b, j, s: (j, 0, 0)),
            vec,
            vec,
            pl.BlockSpec((pl.Element(sr), pl.Element(k)),
                         lambda b, j, s: (pl.multiple_of(side_row0 + step(b, j, s) * sr,
                                                         SUBLANES), 0)),
        ],
        out_specs=[pl.BlockSpec((n_slab, ts, LANES), lambda b, j, s: (j, b * nt + s, 0)),
                   pl.BlockSpec((sr, k), lambda b, j, s: (step(b, j, s), 0))],
        out_shape=[jax.ShapeDtypeStruct((d_lru // LANES, t, LANES), F32),
                   jax.ShapeDtypeStruct((side_rows, k), BF16)],
        scratch_shapes=[pltpu.VMEM((n_slab, SUBLANES, LANES), F32),
                        pltpu.VMEM((n_slab, SUBLANES, LANES), F32)],
        compiler_params=pltpu.CompilerParams(
            dimension_semantics=("arbitrary", "arbitrary", "arbitrary"),
            vmem_limit_bytes=32 * MIB + 2 * sr * k * 6),
        name="lru_mixer",
    )(proj_l, proj_l, conv_w, conv_b, wa, ba, wx, bx, lam, w_flat)


def _out_kernel(ssd_ref, lru_ref, w_ref, x_hbm, g_ref, b_ref, o_hbm, acc_s, xch_s,
                xsem, osem, *, nk_half, alpha):
    i = pl.program_id(0)
    k = pl.program_id(1)
    tm, d_model = acc_s.shape
    n_chunk = tm // LN_ROWS
    n_slot = xch_s.shape[0]

    def x_copy(c, slot):
        rows = pl.ds(pl.multiple_of(i * tm + c * LN_ROWS, LN_ROWS), LN_ROWS)
        return pltpu.make_async_copy(x_hbm.at[rows, :], xch_s.at[slot], xsem.at[slot])

    def o_copy(c):
        src = pl.ds(pl.multiple_of(c * LN_ROWS, LN_ROWS), LN_ROWS)
        dst = pl.ds(pl.multiple_of(i * tm + c * LN_ROWS, LN_ROWS), LN_ROWS)
        return pltpu.make_async_copy(acc_s.at[src, :], o_hbm.at[dst, :], osem)

    def accumulate(lhs, first):
        for n in range(0, d_model, OUT_N_CHUNK):
            sl = slice(n, n + OUT_N_CHUNK)
            part = jnp.dot(lhs, w_ref[:, sl], preferred_element_type=F32)
            if first:
                acc_s[:, sl] = part
            else:
                acc_s[:, sl] += part

    @pl.when(k == 0)
    def _():
        for c in range(n_slot):
            x_copy(c, c).start()
        accumulate(ssd_ref[...], True)

    @pl.when(jnp.logical_and(k > 0, k < nk_half))
    def _():
        accumulate(ssd_ref[...], False)

    @pl.when(k >= nk_half)
    def _():
        lhs = jnp.concatenate([lru_ref[jj] for jj in range(lru_ref.shape[0])], axis=1)
        accumulate(lhs.astype(BF16), False)

    @pl.when(k == 2 * nk_half - 1)
    def _():
        def ln_chunk(c, carry):
            slot = c % n_slot
            x_copy(c, slot).wait()
            rows = pl.ds(pl.multiple_of(c * LN_ROWS, LN_ROWS), LN_ROWS)
            res = acc_s[rows, :] + alpha * xch_s[slot]
            mu = jnp.mean(res, axis=1, keepdims=True)
            cen = res - mu
            var = jnp.mean(cen * cen, axis=1, keepdims=True)
            acc_s[rows, :] = cen * lax.rsqrt(var + NORM_EPS) * g_ref[...] + b_ref[...]
            o_copy(c).start()

            @pl.when(c + n_slot < n_chunk)
            def _():
                x_copy(c + n_slot, slot).start()

            return carry

        lax.fori_loop(0, n_chunk, ln_chunk, 0)

        def drain(c, carry):
            o_copy(c).wait()
            return carry

        lax.fori_loop(0, n_chunk, drain, 0)


def _out_proj(ssd_out, lru_out, w_out, x2d, ln_g, ln_b, *, alpha, tm, tk):
    t, d_half = ssd_out.shape
    d_model = x2d.shape[1]
    nk_half = d_half // tk
    ks = tk // LANES
    n_slot = min(OUT_X_SLOTS, tm // LN_ROWS)
    vmem = (2 * (tm * tk * 2 + tm * tk * 4 + tk * d_model * 2) + tm * d_model * 4
            + n_slot * LN_ROWS * d_model * 4 + tm * tk * 2 + tm * OUT_N_CHUNK * 4 + 2 * MIB)
    return pl.pallas_call(
        functools.partial(_out_kernel, nk_half=nk_half, alpha=alpha),
        grid=(t // tm, 2 * nk_half),
        in_specs=[
            pl.BlockSpec((tm, tk), lambda i, k: (i, jnp.minimum(k, nk_half - 1))),
            pl.BlockSpec((ks, tm, LANES), lambda i, k: (jnp.maximum(k - nk_half, 0), i, 0)),
            pl.BlockSpec((tk, d_model), lambda i, k: (k, 0)),
            pl.BlockSpec(memory_space=pl.ANY),
            pl.BlockSpec((1, d_model), lambda i, k: (0, 0)),
            pl.BlockSpec((1, d_model), lambda i, k: (0, 0)),
        ],
        out_specs=pl.BlockSpec(memory_space=pl.ANY),
        out_shape=jax.ShapeDtypeStruct((t, d_model), F32),
        scratch_shapes=[pltpu.VMEM((tm, d_model), F32),
                        pltpu.VMEM((n_slot, LN_ROWS, d_model), F32),
                        pltpu.SemaphoreType.DMA((n_slot,)),
                        pltpu.SemaphoreType.DMA(())],
        compiler_params=pltpu.CompilerParams(
            dimension_semantics=("arbitrary", "arbitrary"),
            vmem_limit_bytes=vmem),
        name="out_proj_ln",
    )(ssd_out, lru_out, w_out, x2d, ln_g, ln_b)


def _layer(x2d, layer, w_in, ssd_conv_w, ssd_conv_b, ssd_dt_bias, ssd_a_log, ssd_d,
           ssd_norm_w, lru_conv_w, lru_conv_b, lru_wa, lru_ba, lru_wx, lru_bx, lru_lambda,
           w_out, ln_g, ln_b, *, batch, seq, alpha):
    t, d_model = x2d.shape
    d_ssm = d_model
    d_lru = d_model
    heads = d_ssm // SSM_HEADDIM
    d_xbc = d_ssm + 2 * SSM_GROUPS * SSM_STATE
    n_a = d_ssm + d_xbc
    assert heads <= LANES and seq % SSM_CHUNK == 0

    w_t = jnp.swapaxes(w_in, 1, 2)

    tm = min(1024, t)
    xb, dt_raw = _xcast_dt(x2d, w_t, layer=layer, row0=n_a, tm=min(512, t))

    proj_l = _matmul_staged_slab(xb, w_t, layer=layer, row0=n_a + heads, n_cols=2 * d_lru,
                                 tm=tm, tn=512)
    lru_out, w_a = _lru_mixer(
        proj_l, lru_conv_w, lru_conv_b.reshape(1, d_lru),
        (0.5 * lru_wa).astype(BF16), 0.5 * lru_ba.reshape(1, d_lru),
        (0.5 * lru_wx).astype(BF16), 0.5 * lru_bx.reshape(1, d_lru),
        lru_lambda.reshape(1, d_lru), w_t.reshape(-1, d_model),
        side_row0=layer * w_t.shape[1], side_rows=n_a,
        batch=batch, seq=seq, d_lru=d_lru, ts=min(1024, seq), cbw=512)

    proj_a = _matmul_bf16(xb, w_a, tm=tm, tn=1024)
    pad_h = (0, LANES - heads)
    ssd_out, w_out_b = _ssd_mixer(
        proj_a, dt_raw, ssd_conv_w, ssd_conv_b.reshape(1, d_xbc),
        jnp.pad(ssd_dt_bias, pad_h).reshape(1, LANES),
        jnp.pad(ssd_a_log, pad_h).reshape(1, LANES),
        jnp.repeat(ssd_d, SSM_HEADDIM).reshape(1, d_ssm),
        ssd_norm_w.reshape(1, d_ssm), w_out,
        batch=batch, seq=seq, d_ssm=d_ssm)

    return _out_proj(ssd_out, lru_out, w_out_b, x2d,
                     ln_g.reshape(1, d_model), ln_b.reshape(1, d_model),
                     alpha=alpha, tm=min(1024, t), tk=1024)


def kernel(x, w_in, ssd_conv_w, ssd_conv_b, ssd_dt_bias, ssd_a_log, ssd_d, ssd_norm_w,
           lru_conv_w, lru_conv_b, lru_wa, lru_ba, lru_wx, lru_bx, lru_lambda,
           w_out, ln_g, ln_b):
    batch, seq, d_model = x.shape
    depth = w_in.shape[0]
    alpha = (2.0 * depth) ** 0.25
    h = x.reshape(batch * seq, d_model)
    for layer in range(depth):
        h = _layer(h, layer, w_in, ssd_conv_w[layer], ssd_conv_b[layer],
                   ssd_dt_bias[layer], ssd_a_log[layer], ssd_d[layer], ssd_norm_w[layer],
                   lru_conv_w[layer], lru_conv_b[layer], lru_wa[layer], lru_ba[layer],
                   lru_wx[layer], lru_bx[layer], lru_lambda[layer],
                   w_out[layer], ln_g[layer], ln_b[layer],
                   batch=batch, seq=seq, alpha=alpha)
    return h.reshape(batch, seq, d_model)
```

```python
import functools

import jax
import jax.numpy as jnp
from jax import lax
from jax.experimental import pallas as pl
from jax.experimental.pallas import tpu as pltpu

F32 = jnp.float32
BF16 = jnp.bfloat16

SSM_HEADDIM = 64
SSM_GROUPS = 8
SSM_STATE = 128
SSM_CHUNK = 128
LRU_BLOCK = 256
LRU_C = 8.0
CONV_WIDTH = 4
NORM_EPS = 1e-5
LOG2E = 1.4426950408889634

LANES = 128
SUBLANES = 8
MIB = 1024 * 1024
OUT_N_CHUNK = 512
LN_ROWS = 64
OUT_X_SLOTS = 8
LRU_GROUP = SUBLANES * SUBLANES
SSD_GROUPS_PER_STEP = 8

def _softplus(v):
    return jnp.maximum(v, 0.0) + jnp.log1p(jnp.exp(-jnp.abs(v)))


def _mm_staged_kernel(x_ref, w_ref, o_ref, wb_s):
    @pl.when(pl.program_id(1) == 0)
    def _():
        wb_s[...] = w_ref[...].astype(BF16)

    res = lax.dot_general(x_ref[...], wb_s[...], (((1,), (1,)), ((), ())),
                          preferred_element_type=F32)
    for jj in range(o_ref.shape[0]):
        o_ref[jj] = res[:, jj * LANES:(jj + 1) * LANES]


def _matmul_staged_slab(x, w_t, *, layer, row0, n_cols, tm, tn):
    m, k = x.shape
    assert row0 % SUBLANES == 0 and n_cols % tn == 0 and m % tm == 0 and tn % LANES == 0
    row0 = layer * w_t.shape[1] + row0
    w_spec = pl.BlockSpec((pl.Element(tn), pl.Element(k)),
                          lambda j, i: (pl.multiple_of(row0 + j * tn, SUBLANES), 0))
    vmem = (2 * (tm * k * 2 + tn * k * 4 + tm * tn * 4) + tn * k * 2 + tm * tn * 4
            + 6 * MIB)
    return pl.pallas_call(
        _mm_staged_kernel,
        grid=(n_cols // tn, m // tm),
        in_specs=[pl.BlockSpec((tm, k), lambda j, i: (i, 0)), w_spec],
        out_specs=pl.BlockSpec((tn // LANES, tm, LANES), lambda j, i: (j, i, 0)),
        out_shape=jax.ShapeDtypeStruct((n_cols // LANES, m, LANES), F32),
        scratch_shapes=[pltpu.VMEM((tn, k), BF16)],
        compiler_params=pltpu.CompilerParams(
            dimension_semantics=("arbitrary", "arbitrary"),
            vmem_limit_bytes=vmem),
        name="in_proj_lru",
    )(x, w_t.reshape(-1, k))


def _mm_bf16_kernel(x_ref, w_ref, o_ref):
    o_ref[...] = lax.dot_general(x_ref[...], w_ref[...], (((1,), (1,)), ((), ())),
                                 preferred_element_type=F32)


def _matmul_bf16(x, w_b, *, tm, tn):
    m, k = x.shape
    n = w_b.shape[0]
    assert n % tn == 0 and m % tm == 0 and tn % LANES == 0
    vmem = 2 * (tm * k * 2 + tn * k * 2 + tm * tn * 4) + tm * tn * 4 + 6 * MIB
    return pl.pallas_call(
        _mm_bf16_kernel,
        grid=(n // tn, m // tm),
        in_specs=[pl.BlockSpec((tm, k), lambda j, i: (i, 0)),
                  pl.BlockSpec((tn, k), lambda j, i: (j, 0))],
        out_specs=pl.BlockSpec((tm, tn), lambda j, i: (i, j)),
        out_shape=jax.ShapeDtypeStruct((m, n), F32),
        compiler_params=pltpu.CompilerParams(
            dimension_semantics=("arbitrary", "arbitrary"),
            vmem_limit_bytes=vmem),
        name="in_proj",
    )(x, w_b)


def _xcast_dt_kernel(x_ref, w_ref, xb_ref, dt_ref, wb_s):
    @pl.when(pl.program_id(0) == 0)
    def _():
        wb_s[...] = w_ref[...].T.astype(BF16)

    xb = x_ref[...].astype(BF16)
    xb_ref[...] = xb
    dt_ref[...] = jnp.dot(xb, wb_s[...], preferred_element_type=F32)


def _xcast_dt(x2d, w_t, *, layer, row0, tm):
    m, k = x2d.shape
    assert row0 % SUBLANES == 0 and m % tm == 0
    row0 = layer * w_t.shape[1] + row0
    vmem = 2 * (tm * k * 4 + tm * k * 2 + LANES * k * 4 + tm * LANES * 4) + 8 * MIB
    return pl.pallas_call(
        _xcast_dt_kernel,
        grid=(m // tm,),
        in_specs=[pl.BlockSpec((tm, k), lambda i: (i, 0)),
                  pl.BlockSpec((pl.Element(LANES), pl.Element(k)), lambda i: (row0, 0))],
        out_specs=[pl.BlockSpec((tm, k), lambda i: (i, 0)),
                   pl.BlockSpec((tm, LANES), lambda i: (i, 0))],
        out_shape=[jax.ShapeDtypeStruct((m, k), BF16),
                   jax.ShapeDtypeStruct((m, LANES), F32)],
        scratch_shapes=[pltpu.VMEM((k, LANES), BF16)],
        compiler_params=pltpu.CompilerParams(
            dimension_semantics=("arbitrary",), vmem_limit_bytes=vmem),
        name="xcast_dt",
    )(x2d, w_t.reshape(-1, k))


def _ssd_kernel(z_ref, xs_ref, b_ref, c_ref, dt_ref,
                cwx_ref, cwb_ref, cwc_ref, cbx_ref, cbb_ref, cbc_ref,
                dtb_ref, alog_ref, d_ref, nw_ref, wout_ref,
                o_ref, woutb_ref,
                dt_s, acs_s, acst_s, state_s, tail_s, ext_s):
    c = pl.program_id(1)
    gstep = pl.program_id(2)
    L = SSM_CHUNK
    gs = SSD_GROUPS_PER_STEP

    woutb_ref[...] = wout_ref[...].astype(BF16)
    gw = xs_ref.shape[1] // gs

    row_id = lax.broadcasted_iota(jnp.int32, (L, LANES), 0)

    @pl.when(gstep == 0)
    def _():
        dt = _softplus(dt_ref[...] + dtb_ref[...])
        a_dt = dt * (-jnp.exp(alog_ref[...]))
        acs = a_dt
        k = 1
        while k < L:
            acs = acs + jnp.where(row_id >= k, pltpu.roll(acs, k, axis=0), 0.0)
            k *= 2
        acs2 = acs * LOG2E
        dt_s[...] = dt
        acs_s[...] = acs2
        acst_s[...] = acs2.T

    for gg in range(gs):
        _ssd_group(gg, gstep * gs + gg, c, gw,
                   z_ref, xs_ref, b_ref, c_ref, cwx_ref, cwb_ref, cwc_ref,
                   cbx_ref, cbb_ref, cbc_ref, d_ref, nw_ref, o_ref,
                   dt_s, acs_s, acst_s, state_s, tail_s, ext_s)


def _ssd_group(gg, g, c, gw, z_ref, xs_ref, b_ref, c_ref, cwx_ref, cwb_ref, cwc_ref,
               cbx_ref, cbb_ref, cbc_ref, d_ref, nw_ref, o_ref,
               dt_s, acs_s, acst_s, state_s, tail_s, ext_s):
    L = SSM_CHUNK
    heads_per_group = gw // SSM_HEADDIM
    pairs = gw // LANES
    n_xs = gw // LANES
    x0 = gg * gw
    bc0 = gg * SSM_STATE
    row_id = lax.broadcasted_iota(jnp.int32, (L, LANES), 0)
    lane_id = lax.broadcasted_iota(jnp.int32, (L, LANES), 1)

    @pl.when(c == 0)
    def _():
        state_s[g] = jnp.zeros(state_s.shape[1:], F32)
        tail_s[g] = jnp.zeros(tail_s.shape[1:], F32)

    acts = []
    for s in range(n_xs + 2):
        if s < n_xs:
            lanes = slice(x0 + s * LANES, x0 + (s + 1) * LANES)
            in_ref, w_ref, bias_ref = xs_ref, cwx_ref, cbx_ref
        else:
            lanes = slice(bc0, bc0 + LANES)
            in_ref, w_ref, bias_ref = ((b_ref, cwb_ref, cbb_ref) if s == n_xs
                                       else (c_ref, cwc_ref, cbc_ref))
        raw = in_ref[:, lanes]
        slab = gg * (n_xs + 2) + s
        ext_s[slab, 0:SUBLANES, :] = tail_s[g, :, s * LANES:(s + 1) * LANES]
        ext_s[slab, SUBLANES:SUBLANES + L, :] = raw
        tail_s[g, :, s * LANES:(s + 1) * LANES] = raw[L - SUBLANES:, :]
        acc = raw * w_ref[CONV_WIDTH - 1:CONV_WIDTH, lanes] + bias_ref[:, lanes]
        for k in range(1, CONV_WIDTH):
            acc = acc + (ext_s[slab, pl.ds(SUBLANES - k, L), :]
                         * w_ref[CONV_WIDTH - 1 - k:CONV_WIDTH - k, lanes])
        half = 0.5 * acc
        acts.append(half + half * jnp.tanh(half))
    xs = jnp.concatenate(acts[:n_xs], axis=1)
    bm = acts[n_xs].astype(BF16)
    cm = acts[n_xs + 1].astype(BF16)

    cb_mat = lax.dot_general(cm, bm, (((1,), (1,)), ((), ())),
                             preferred_element_type=F32)
    s_prev = state_s[g]
    y_off = jnp.dot(cm, s_prev.astype(BF16), preferred_element_type=F32)

    dt_all = dt_s[...]
    acs_all = acs_s[...]
    causal = row_id >= lane_id
    low = lane_id < SSM_HEADDIM

    def head_col(arr, idx):
        return jnp.sum(jnp.where(lane_id == idx, arr, 0.0), axis=1, keepdims=True)

    def head_pair(arr, idx0):
        return jnp.where(low, head_col(arr, idx0), head_col(arr, idx0 + 1))

    y_parts, xdec_parts, cdec_parts = [], [], []
    for j in range(pairs):
        i0 = g * heads_per_group + 2 * j
        i1 = i0 + 1
        col0 = head_col(acs_all, i0)
        col1 = head_col(acs_all, i1)
        row0 = acst_s[pl.ds(i0, 1), :]
        row1 = acst_s[pl.ds(i1, 1), :]
        l0 = jnp.where(causal, jnp.exp2(col0 - row0), 0.0)
        l1 = jnp.where(causal, jnp.exp2(col1 - row1), 0.0)
        m_pair = jnp.concatenate([cb_mat * l0, cb_mat * l1], axis=1).astype(BF16)

        dt_pair = head_pair(dt_all, i0)
        xs_pair = xs[:, j * LANES:(j + 1) * LANES]
        xt = xs_pair * dt_pair
        xt_b = xt.astype(BF16)
        zero = jnp.zeros_like(xt_b)
        rhs = jnp.concatenate([jnp.where(low, xt_b, zero),
                               jnp.where(low, zero, xt_b)], axis=0)
        y_diag = jnp.dot(m_pair, rhs, preferred_element_type=F32)

        scale = jnp.where(low, jnp.exp2(col0), jnp.exp2(col1))
        last0 = col0[L - 1:L, :]
        last1 = col1[L - 1:L, :]
        dec = jnp.where(low, jnp.exp2(last0 - col0), jnp.exp2(last1 - col1))
        y_parts.append(y_diag + y_off[:, j * LANES:(j + 1) * LANES] * scale
                       + d_ref[:, x0 + j * LANES:x0 + (j + 1) * LANES] * xs_pair)
        xdec_parts.append((xt * dec).astype(BF16))
        cdec_parts.append(scale[L - 1:L, :])

    xdec = jnp.concatenate(xdec_parts, axis=1)
    cdec = jnp.concatenate(cdec_parts, axis=1)
    new_states = lax.dot_general(bm, xdec, (((0,), (0,)), ((), ())),
                                 preferred_element_type=F32)
    state_s[g] = s_prev * cdec + new_states

    y = jnp.concatenate(y_parts, axis=1)
    zh = 0.5 * z_ref[:, x0:x0 + gw]
    y = y * (zh + zh * jnp.tanh(zh))
    ms = jnp.mean(y * y, axis=1, keepdims=True)
    o_ref[:, x0:x0 + gw] = (y * lax.rsqrt(ms + NORM_EPS)
                            * nw_ref[:, x0:x0 + gw]).astype(o_ref.dtype)


def _ssd_mixer(proj_a, dt_raw, conv_w, conv_b, dt_bias, a_log, d_exp, norm_w, w_out,
               *, batch, seq, d_ssm):
    L = SSM_CHUNK
    gs = SSD_GROUPS_PER_STEP
    gw = d_ssm // SSM_GROUPS
    xw = gs * gw
    sw = gs * SSM_STATE
    nchunk = seq // L
    t = batch * seq
    assert SSM_GROUPS % gs == 0
    n_gstep = SSM_GROUPS // gs
    n_step = batch * nchunk * n_gstep
    w_rows, w_cols = w_out.shape[0] // n_step, w_out.shape[1]
    assert w_out.shape[0] % n_step == 0 and w_rows % (2 * SUBLANES) == 0

    def step(b, c, g):
        return (b * nchunk + c) * n_gstep + g

    xs0 = d_ssm // xw
    b0 = 2 * d_ssm // sw
    c0 = b0 + SSM_GROUPS // gs
    cwb0 = d_ssm // sw
    cwc0 = cwb0 + SSM_GROUPS // gs

    def rows(b, c, g):
        return b * nchunk + c

    in_specs = [
        pl.BlockSpec((L, xw), lambda b, c, g: (rows(b, c, g), g)),
        pl.BlockSpec((L, xw), lambda b, c, g: (rows(b, c, g), xs0 + g)),
        pl.BlockSpec((L, sw), lambda b, c, g: (rows(b, c, g), b0 + g)),
        pl.BlockSpec((L, sw), lambda b, c, g: (rows(b, c, g), c0 + g)),
        pl.BlockSpec((L, LANES), lambda b, c, g: (rows(b, c, g), 0)),
        pl.BlockSpec((CONV_WIDTH, xw), lambda b, c, g: (0, g)),
        pl.BlockSpec((CONV_WIDTH, sw), lambda b, c, g: (0, cwb0 + g)),
        pl.BlockSpec((CONV_WIDTH, sw), lambda b, c, g: (0, cwc0 + g)),
        pl.BlockSpec((1, xw), lambda b, c, g: (0, g)),
        pl.BlockSpec((1, sw), lambda b, c, g: (0, cwb0 + g)),
        pl.BlockSpec((1, sw), lambda b, c, g: (0, cwc0 + g)),
        pl.BlockSpec((1, LANES), lambda b, c, g: (0, 0)),
        pl.BlockSpec((1, LANES), lambda b, c, g: (0, 0)),
        pl.BlockSpec((1, xw), lambda b, c, g: (0, g)),
        pl.BlockSpec((1, xw), lambda b, c, g: (0, g)),
        pl.BlockSpec((w_rows, w_cols), lambda b, c, g: (step(b, c, g), 0)),
    ]
    return pl.pallas_call(
        _ssd_kernel,
        grid=(batch, nchunk, n_gstep),
        in_specs=in_specs,
        out_specs=[pl.BlockSpec((L, xw), lambda b, c, g: (rows(b, c, g), g)),
                   pl.BlockSpec((w_rows, w_cols), lambda b, c, g: (step(b, c, g), 0))],
        out_shape=[jax.ShapeDtypeStruct((t, d_ssm), BF16),
                   jax.ShapeDtypeStruct(w_out.shape, BF16)],
        scratch_shapes=[
            pltpu.VMEM((L, LANES), F32),
            pltpu.VMEM((L, LANES), F32),
            pltpu.VMEM((LANES, L), F32),
            pltpu.VMEM((SSM_GROUPS, SSM_STATE, gw), F32),
            pltpu.VMEM((SSM_GROUPS, SUBLANES, gw + 2 * SSM_STATE), F32),
            pltpu.VMEM((gs * (gw // LANES + 2), SUBLANES + L, LANES), F32),
        ],
        compiler_params=pltpu.CompilerParams(
            dimension_semantics=("arbitrary", "arbitrary", "arbitrary"),
            vmem_limit_bytes=32 * MIB + 2 * w_rows * w_cols * 6),
        name="ssd_mixer",
    )(proj_a, proj_a, proj_a, proj_a, dt_raw,
      conv_w, conv_w, conv_w, conv_b, conv_b, conv_b,
      dt_bias, a_log, d_exp, norm_w, w_out)


def _lru_kernel(lx_ref, lg_ref, cw_ref, cb_ref, wa_ref, ba_ref, wx_ref, bx_ref,
                lam_ref, wside_ref, o_ref, wside_out_ref, tail_s, h_s):
    tb = pl.program_id(2)

    wside_out_ref[...] = wside_ref[...].astype(BF16)
    n_slab, ts, _ = lx_ref.shape
    n_group = ts // LRU_GROUP
    slabs_per_head = LRU_BLOCK // LANES
    S = SUBLANES
    vshape = (S, LANES)

    @pl.when(tb == 0)
    def _():
        tail_s[...] = jnp.zeros(tail_s.shape, F32)
        h_s[...] = jnp.zeros(h_s.shape, F32)

    sub = lax.broadcasted_iota(jnp.int32, vshape, 0)
    seg0 = sub == 0

    def seg_rows(ref, jj, row0):
        return ref[jj, pl.ds(row0, S, stride=S), :]

    u = []
    for jj in range(n_slab):
        lanes = slice(jj * LANES, (jj + 1) * LANES)
        taps = [jnp.broadcast_to(cw_ref[k:k + 1, lanes], vshape) for k in range(CONV_WIDTH)]
        bias = jnp.broadcast_to(cb_ref[:, lanes], vshape)
        u_slab = []
        for g in range(n_group):
            g0 = g * LRU_GROUP
            x_t = [seg_rows(lx_ref, jj, g0 + t) for t in range(S)]
            before = []
            for d in range(CONV_WIDTH - 1, 0, -1):
                if g == 0:
                    prev_row = jnp.broadcast_to(tail_s[jj, S - d:S - d + 1, :], vshape)
                    before.append(jnp.where(seg0, prev_row,
                                            pltpu.roll(x_t[S - d], 1, axis=0)))
                else:
                    before.append(seg_rows(lx_ref, jj, g0 - d))
            ext = before + x_t
            u_g = []
            for t in range(S):
                acc = ext[t + 3] * taps[3] + bias
                for k in range(CONV_WIDTH - 1):
                    acc = acc + ext[t + k] * taps[k]
                u_g.append(acc)
            u_slab.append(u_g)
        tail_s[jj] = lx_ref[jj, ts - S:ts, :]
        u.append(u_slab)

    pre_r, pre_i = [], []
    for h in range(n_slab // slabs_per_head):
        lhs = jnp.concatenate(
            [jnp.concatenate([u[h * slabs_per_head + q][g][t] for q in range(slabs_per_head)],
                             axis=1)
             for g in range(n_group) for t in range(S)], axis=0).astype(BF16)
        pre_r.append(jnp.dot(lhs, wa_ref[h], preferred_element_type=F32))
        pre_i.append(jnp.dot(lhs, wx_ref[h], preferred_element_type=F32))

    for jj in range(n_slab):
        lanes = slice(jj * LANES, (jj + 1) * LANES)
        h_idx, q = divmod(jj, slabs_per_head)
        qlanes = slice(q * LANES, (q + 1) * LANES)
        half_coef = jnp.broadcast_to(-0.5 * LRU_C * _softplus(-lam_ref[:, lanes]), vshape)
        b_r = jnp.broadcast_to(ba_ref[:, lanes], vshape)
        b_i = jnp.broadcast_to(bx_ref[:, lanes], vshape)
        carry = h_s[jj]
        for g in range(n_group):
            g0 = g * LRU_GROUP
            a_t, h_t = [], []
            for t in range(S):
                r0 = g0 + t * S
                t_r = jnp.tanh(pre_r[h_idx][r0:r0 + S, qlanes] + b_r)
                t_i = jnp.tanh(pre_i[h_idx][r0:r0 + S, qlanes] + b_i)
                log_a = half_coef * t_r + half_coef
                a = jnp.exp(log_a)
                th = jnp.tanh(log_a)
                q4 = (-0.5 * th) / (1.0 - th)
                root = jnp.where(q4 > 0.0, q4 * lax.rsqrt(q4), 0.0)
                bv = root * ((t_i + 1.0) * u[jj][g][t])
                if t == 0:
                    a_t.append(a)
                    h_t.append(bv)
                else:
                    a_t.append(a * a_t[-1])
                    h_t.append(a * h_t[-1] + bv)
            a_e, h_e = a_t[-1], h_t[-1]
            k = 1
            while k < S:
                keep = sub >= k
                a_sh = jnp.where(keep, pltpu.roll(a_e, k, axis=0), 1.0)
                h_sh = jnp.where(keep, pltpu.roll(h_e, k, axis=0), 0.0)
                h_e = a_e * h_sh + h_e
                a_e = a_e * a_sh
                k *= 2
            ends = h_e + a_e * carry
            seg_in = jnp.where(seg0, carry, pltpu.roll(ends, 1, axis=0))
            carry = jnp.broadcast_to(ends[S - 1:S, :], vshape)
            for t in range(S):
                hv = h_t[t] + a_t[t] * seg_in
                lh = 0.5 * seg_rows(lg_ref, jj, g0 + t)
                silu = lh + lh * jnp.tanh(lh)
                o_ref[jj, pl.ds(g0 + t, S, stride=S), :] = hv * silu
        h_s[jj] = carry


def _lru_mixer(proj_l, conv_w, conv_b, wa, ba, wx, bx, lam, w_flat, *, side_row0, side_rows,
               batch, seq, d_lru, ts, cbw):
    t = batch * seq
    nt = seq // ts
    ncb = d_lru // cbw
    n_slab = cbw // LANES
    hb = cbw // LRU_BLOCK
    n_step = batch * ncb * nt
    k = w_flat.shape[1]
    sr = side_rows // n_step
    assert ts % LRU_GROUP == 0 and seq % ts == 0
    assert side_rows % n_step == 0 and sr % (2 * SUBLANES) == 0 and side_row0 % SUBLANES == 0

    def step(b, j, s):
        return (b * ncb + j) * nt + s

    vec = pl.BlockSpec((1, cbw), lambda b, j, s: (0, j))
    return pl.pallas_call(
        _lru_kernel,
        grid=(batch, ncb, nt),
        in_specs=[
            pl.BlockSpec((n_slab, ts, LANES), lambda b, j, s: (j, b * nt + s, 0)),
            pl.BlockSpec((n_slab, ts, LANES), lambda b, j, s: (ncb + j, b * nt + s, 0)),
            pl.BlockSpec((CONV_WIDTH, cbw), lambda b, j, s: (0, j)),
            vec,
            pl.BlockSpec((hb, LRU_BLOCK, LRU_BLOCK), lambda b, j, s: (j, 0, 0)),
            vec,
            pl.BlockSpec((hb, LRU_BLOCK, LRU_BLOCK), lambda b, j, s: (j, 0, 0)),
            vec,
            vec,
            pl.BlockSpec((pl.Element(sr), pl.Element(k)),
                         lambda b, j, s: (pl.multiple_of(side_row0 + step(b, j, s) * sr,
                                                         SUBLANES), 0)),
        ],
        out_specs=[pl.BlockSpec((n_slab, ts, LANES), lambda b, j, s: (j, b * nt + s, 0)),
                   pl.BlockSpec((sr, k), lambda b, j, s: (step(b, j, s), 0))],
        out_shape=[jax.ShapeDtypeStruct((d_lru // LANES, t, LANES), F32),
                   jax.ShapeDtypeStruct((side_rows, k), BF16)],
        scratch_shapes=[pltpu.VMEM((n_slab, SUBLANES, LANES), F32),
                        pltpu.VMEM((n_slab, SUBLANES, LANES), F32)],
        compiler_params=pltpu.CompilerParams(
            dimension_semantics=("arbitrary", "arbitrary", "arbitrary"),
            vmem_limit_bytes=32 * MIB + 2 * sr * k * 6),
        name="lru_mixer",
    )(proj_l, proj_l, conv_w, conv_b, wa, ba, wx, bx, lam, w_flat)


def _out_kernel(ssd_ref, lru_ref, w_ref, x_hbm, g_ref, b_ref, o_hbm, acc_s, xch_s,
                xsem, osem, *, nk_half, alpha):
    i = pl.program_id(0)
    k = pl.program_id(1)
    tm, d_model = acc_s.shape
    n_chunk = tm // LN_ROWS
    n_slot = xch_s.shape[0]

    def x_copy(c, slot):
        rows = pl.ds(pl.multiple_of(i * tm + c * LN_ROWS, LN_ROWS), LN_ROWS)
        return pltpu.make_async_copy(x_hbm.at[rows, :], xch_s.at[slot], xsem.at[slot])

    def o_copy(c):
        src = pl.ds(pl.multiple_of(c * LN_ROWS, LN_ROWS), LN_ROWS)
        dst = pl.ds(pl.multiple_of(i * tm + c * LN_ROWS, LN_ROWS), LN_ROWS)
        return pltpu.make_async_copy(acc_s.at[src, :], o_hbm.at[dst, :], osem)

    def accumulate(lhs, first):
        for n in range(0, d_model, OUT_N_CHUNK):
            sl = slice(n, n + OUT_N_CHUNK)
            part = jnp.dot(lhs, w_ref[:, sl], preferred_element_type=F32)
            if first:
                acc_s[:, sl] = part
            else:
                acc_s[:, sl] += part

    @pl.when(k == 0)
    def _():
        for c in range(n_slot):
            x_copy(c, c).start()
        accumulate(ssd_ref[...], True)

    @pl.when(jnp.logical_and(k > 0, k < nk_half))
    def _():
        accumulate(ssd_ref[...], False)

    @pl.when(k >= nk_half)
    def _():
        lhs = jnp.concatenate([lru_ref[jj] for jj in range(lru_ref.shape[0])], axis=1)
        accumulate(lhs.astype(BF16), False)

    @pl.when(k == 2 * nk_half - 1)
    def _():
        def ln_chunk(c, carry):
            slot = c % n_slot
            x_copy(c, slot).wait()
            rows = pl.ds(pl.multiple_of(c * LN_ROWS, LN_ROWS), LN_ROWS)
            res = acc_s[rows, :] + alpha * xch_s[slot]
            mu = jnp.mean(res, axis=1, keepdims=True)
            cen = res - mu
            var = jnp.mean(cen * cen, axis=1, keepdims=True)
            acc_s[rows, :] = cen * lax.rsqrt(var + NORM_EPS) * g_ref[...] + b_ref[...]
            o_copy(c).start()

            @pl.when(c + n_slot < n_chunk)
            def _():
                x_copy(c + n_slot, slot).start()

            return carry

        lax.fori_loop(0, n_chunk, ln_chunk, 0)

        def drain(c, carry):
            o_copy(c).wait()
            return carry

        lax.fori_loop(0, n_chunk, drain, 0)


def _out_proj(ssd_out, lru_out, w_out, x2d, ln_g, ln_b, *, alpha, tm, tk):
    t, d_half = ssd_out.shape
    d_model = x2d.shape[1]
    nk_half = d_half // tk
    ks = tk // LANES
    n_slot = min(OUT_X_SLOTS, tm // LN_ROWS)
    vmem = (2 * (tm * tk * 2 + tm * tk * 4 + tk * d_model * 2) + tm * d_model * 4
            + n_slot * LN_ROWS * d_model * 4 + tm * tk * 2 + tm * OUT_N_CHUNK * 4 + 2 * MIB)
    return pl.pallas_call(
        functools.partial(_out_kernel, nk_half=nk_half, alpha=alpha),
        grid=(t // tm, 2 * nk_half),
        in_specs=[
            pl.BlockSpec((tm, tk), lambda i, k: (i, jnp.minimum(k, nk_half - 1))),
            pl.BlockSpec((ks, tm, LANES), lambda i, k: (jnp.maximum(k - nk_half, 0), i, 0)),
            pl.BlockSpec((tk, d_model), lambda i, k: (k, 0)),
            pl.BlockSpec(memory_space=pl.ANY),
            pl.BlockSpec((1, d_model), lambda i, k: (0, 0)),
            pl.BlockSpec((1, d_model), lambda i, k: (0, 0)),
        ],
        out_specs=pl.BlockSpec(memory_space=pl.ANY),
        out_shape=jax.ShapeDtypeStruct((t, d_model), F32),
        scratch_shapes=[pltpu.VMEM((tm, d_model), F32),
                        pltpu.VMEM((n_slot, LN_ROWS, d_model), F32),
                        pltpu.SemaphoreType.DMA((n_slot,)),
                        pltpu.SemaphoreType.DMA(())],
        compiler_params=pltpu.CompilerParams(
            dimension_semantics=("arbitrary", "arbitrary"),
            vmem_limit_bytes=vmem),
        name="out_proj_ln",
    )(ssd_out, lru_out, w_out, x2d, ln_g, ln_b)


def _layer(x2d, layer, w_in, ssd_conv_w, ssd_conv_b, ssd_dt_bias, ssd_a_log, ssd_d,
           ssd_norm_w, lru_conv_w, lru_conv_b, lru_wa, lru_ba, lru_wx, lru_bx, lru_lambda,
           w_out, ln_g, ln_b, *, batch, seq, alpha):
    t, d_model = x2d.shape
    d_ssm = d_model
    d_lru = d_model
    heads = d_ssm // SSM_HEADDIM
    d_xbc = d_ssm + 2 * SSM_GROUPS * SSM_STATE
    n_a = d_ssm + d_xbc
    assert heads <= LANES and seq % SSM_CHUNK == 0

    w_t = jnp.swapaxes(w_in, 1, 2)

    tm = min(1024, t)
    xb, dt_raw = _xcast_dt(x2d, w_t, layer=layer, row0=n_a, tm=min(512, t))

    proj_l = _matmul_staged_slab(xb, w_t, layer=layer, row0=n_a + heads, n_cols=2 * d_lru,
                                 tm=tm, tn=512)
    lru_out, w_a = _lru_mixer(
        proj_l, lru_conv_w, lru_conv_b.reshape(1, d_lru),
        (0.5 * lru_wa).astype(BF16), 0.5 * lru_ba.reshape(1, d_lru),
        (0.5 * lru_wx).astype(BF16), 0.5 * lru_bx.reshape(1, d_lru),
        lru_lambda.reshape(1, d_lru), w_t.reshape(-1, d_model),
        side_row0=layer * w_t.shape[1], side_rows=n_a,
        batch=batch, seq=seq, d_lru=d_lru, ts=min(1024, seq), cbw=512)

    proj_a = _matmul_bf16(xb, w_a, tm=tm, tn=1024)
    pad_h = (0, LANES - heads)
    ssd_out, w_out_b = _ssd_mixer(
        proj_a, dt_raw, ssd_conv_w, ssd_conv_b.reshape(1, d_xbc),
        jnp.pad(ssd_dt_bias, pad_h).reshape(1, LANES),
        jnp.pad(ssd_a_log, pad_h).reshape(1, LANES),
        jnp.repeat(ssd_d, SSM_HEADDIM).reshape(1, d_ssm),
        ssd_norm_w.reshape(1, d_ssm), w_out,
        batch=batch, seq=seq, d_ssm=d_ssm)

    return _out_proj(ssd_out, lru_out, w_out_b, x2d,
                     ln_g.reshape(1, d_model), ln_b.reshape(1, d_model),
                     alpha=alpha, tm=min(1024, t), tk=1024)


def kernel(x, w_in, ssd_conv_w, ssd_conv_b, ssd_dt_bias, ssd_a_log, ssd_d, ssd_norm_w,
           lru_conv_w, lru_conv_b, lru_wa, lru_ba, lru_wx, lru_bx, lru_lambda,
           w_out, ln_g, ln_b):
    batch, seq, d_model = x.shape
    depth = w_in.shape[0]
    alpha = (2.0 * depth) ** 0.25
    h = x.reshape(batch * seq, d_model)
    for layer in range(depth):
        h = _layer(h, layer, w_in, ssd_conv_w[layer], ssd_conv_b[layer],
                   ssd_dt_bias[layer], ssd_a_log[layer], ssd_d[layer], ssd_norm_w[layer],
                   lru_conv_w[layer], lru_conv_b[layer], lru_wa[layer], lru_ba[layer],
                   lru_wx[layer], lru_bx[layer], lru_lambda[layer],
                   w_out[layer], ln_g[layer], ln_b[layer],
                   batch=batch, seq=seq, alpha=alpha)
    return h.reshape(batch, seq, d_model)
```
